```python
import math
import jax, jax.numpy as jnp
from jax import lax
import numpy as np

D_MODEL = 1024
BATCH = 16
SEQ = 4096
DEPTH = 1
DEC_BATCH = 8
DEC_SEQ = 16
PAST_LEN = 1024

CHUNK = 64
W_A = D_MODEL // 2
DK = 128
DV = 128
H_A = W_A // DK
W_B = D_MODEL // 2
N_BLK_B = 8
BLK_W_B = W_B // N_BLK_B
CONV_W = 4
RG_C = 8.0
MIX_W = W_A + W_B
IN_COLS = 4 * W_A + 2 * W_B
PEER_HEADS = 8
N_SUBKEYS = 128
N_EXPERTS = N_SUBKEYS * N_SUBKEYS
PEER_TOPK = 16
PEER_DQ = 256
PEER_DHALF = PEER_DQ // 2
PEER_BLOCK = 256
PLE_DIM = 256
EPS = 1e-6

kernel_name = "hgrn2_rglru_peer_streaming_step"


def rmsnorm(x, g):
    x32 = x.astype(jnp.float32)
    y = x32 * lax.rsqrt(jnp.mean(x32 * x32, axis=-1, keepdims=True) + EPS)
    return (y * g.astype(jnp.float32)).astype(x.dtype)


def hgrn2_mixer(zq, zf, zi, lb, S0):
    B_, L = zq.shape[0], zq.shape[1]
    C = min(CHUNK, L)
    nC = L // C
    f = lb + (1.0 - lb) * jax.nn.sigmoid(zf.astype(jnp.float32))
    k = 1.0 - f
    logf = jnp.log(f)
    q = jax.nn.silu(zq.astype(jnp.float32))
    v = zi.astype(jnp.float32)

    def to_chunks(a):
        return a.reshape(B_, nC, C, a.shape[2], a.shape[3]).transpose(1, 0, 3, 2, 4)

    mask = jnp.tril(jnp.ones((C, C), dtype=bool))

    def step(S, inp):
        qc, kc, vc, gc = inp
        cum = jnp.cumsum(gc, axis=2)
        qd = qc * jnp.exp(cum)
        kd = kc * jnp.exp(-cum)
        att = jnp.where(mask, jnp.einsum('bhtd,bhsd->bhts', qd, kd), 0.0)
        o = jnp.einsum('bhts,bhsv->bhtv', att, vc) + jnp.einsum('bhtd,bhdv->bhtv', qd, S)
        last = cum[:, :, -1:, :]
        kdec = kc * jnp.exp(last - cum)
        S = jnp.exp(last[:, :, 0, :])[..., None] * S + jnp.einsum('bhsd,bhsv->bhdv', kdec, vc)
        return S, o

    S, o = lax.scan(step, S0.astype(jnp.float32), (to_chunks(q), to_chunks(k), to_chunks(v), to_chunks(logf)))
    o = o.transpose(1, 0, 3, 2, 4).reshape(B_, L, H_A, DV)
    return o, S


def rglru_mixer(xb, gate, conv_buf, h0, conv_w, conv_b, w_a, b_a, w_x, b_x, lam):
    B_, L = xb.shape[0], xb.shape[1]
    xp = jnp.concatenate([conv_buf.astype(xb.dtype), xb], axis=1)
    xc = conv_b + sum(xp[:, j:j + L] * conv_w[j] for j in range(CONV_W))
    new_buf = xp[:, -(CONV_W - 1):]
    xh = xc.reshape(B_, L, N_BLK_B, BLK_W_B)
    r = jax.nn.sigmoid(jnp.einsum('blnc,ncd->blnd', xh, w_a).reshape(B_, L, W_B) + b_a)
    i = jax.nn.sigmoid(jnp.einsum('blnc,ncd->blnd', xh, w_x).reshape(B_, L, W_B) + b_x)
    log_a = -RG_C * r.astype(jnp.float32) * jax.nn.softplus(-lam.astype(jnp.float32))
    a = jnp.exp(log_a)
    b = jnp.sqrt(-jnp.expm1(2.0 * log_a)) * (i * xc).astype(jnp.float32)
    b = b.at[:, 0].add(a[:, 0] * h0.astype(jnp.float32))

    def combine(lhs, rhs):
        a_l, b_l = lhs
        a_r, b_r = rhs
        return a_l * a_r, a_r * b_l + b_r

    _, h = lax.associative_scan(combine, (a, b), axis=1)
    out = h.astype(xb.dtype) * jax.nn.gelu(gate)
    return out, h[:, -1], new_buf


def peer(x, w_q, keys, U, V):
    B_, L, D = x.shape
    T = B_ * L
    blk = min(PEER_BLOCK, T)
    nblk = -(-T // blk)
    pad = nblk * blk - T
    xt = jnp.pad(x.reshape(T, D), ((0, pad), (0, 0))).reshape(nblk, blk, D)
    keys32 = keys.astype(jnp.float32)

    def one(xb):
        q = (xb @ w_q).reshape(blk, PEER_HEADS, 2, PEER_DHALF).astype(jnp.float32)
        s = jnp.einsum('thpc,hpnc->thpn', q, keys32)
        sv, si = lax.top_k(s, PEER_TOPK)
        cand = (sv[:, :, 0, :, None] + sv[:, :, 1, None, :]).reshape(blk, PEER_HEADS, PEER_TOPK * PEER_TOPK)
        cidx = (si[:, :, 0, :, None] * N_SUBKEYS + si[:, :, 1, None, :]).reshape(blk, PEER_HEADS, PEER_TOPK * PEER_TOPK)
        cs, ci = lax.top_k(cand, PEER_TOPK)
        eidx = jnp.take_along_axis(cidx, ci, axis=-1)
        g = jax.nn.softmax(cs, axis=-1)
        u = U[eidx]
        act = jax.nn.gelu(jnp.einsum('thkd,td->thk', u, xb).astype(jnp.float32))
        wgt = (g * act).astype(xb.dtype)
        return jnp.einsum('thk,thkd->td', wgt, V[eidx])

    out = lax.map(one, xt).reshape(nblk * blk, D)[:T]
    return out.reshape(B_, L, D)


def run_layer(h, p_l, S0, h0, buf0, lb, lp):
    B_, L = h.shape[0], h.shape[1]
    u = rmsnorm(h, lp['g_mix'])
    z = u @ lp['w_in']
    zq, zf, zi, zg, zx, zy = jnp.split(z, [W_A, 2 * W_A, 3 * W_A, 4 * W_A, 4 * W_A + W_B], axis=-1)
    oA, S_new = hgrn2_mixer(zq.reshape(B_, L, H_A, DK), zf.reshape(B_, L, H_A, DK),
                            zi.reshape(B_, L, H_A, DV), lb, S0)
    oA = rmsnorm(oA, lp['g_hgrn_norm'].reshape(H_A, DV)).reshape(B_, L, W_A).astype(h.dtype) * jax.nn.silu(zg)
    oB, h_new, buf_new = rglru_mixer(zx, zy, buf0, h0, lp['conv_w'], lp['conv_b'], lp['w_rg_a'], lp['b_rg_a'],
                                     lp['w_rg_x'], lp['b_rg_x'], lp['lambda_rg'])
    h = h + jnp.concatenate([oA, oB], axis=-1) @ lp['w_out']
    h = h + peer(rmsnorm(h, lp['g_ffn']), lp['w_peer_q'], lp['peer_keys'], lp['peer_u'], lp['peer_v'])
    h = h + (p_l @ lp['w_ple']) * jax.nn.sigmoid(rmsnorm(h, lp['g_ple']) @ lp['w_ple_gate'])
    return h, S_new, h_new, buf_new


def setup_inputs(seed: int = 0) -> dict:
    key = jax.random.key(seed)
    ks = jax.random.split(key, 32)

    def nrm(k, shape, s):
        return jax.random.normal(k, shape, jnp.float32) * s

    a0 = jax.random.uniform(ks[13], (DEPTH, W_B), jnp.float32, minval=0.9, maxval=0.999)
    s0 = a0 ** (1.0 / RG_C)
    lam = jnp.log(s0) - jnp.log1p(-s0)
    return {
        "x_prompt": nrm(ks[0], (BATCH, SEQ, D_MODEL), 1.0),
        "x_sample": nrm(ks[1], (DEC_BATCH, DEC_SEQ, D_MODEL), 1.0),
        "state_hgrn": nrm(ks[2], (DEPTH, DEC_BATCH, H_A, DK, DV), 0.5),
        "state_rglru": nrm(ks[3], (DEPTH, DEC_BATCH, W_B), 0.5),
        "state_conv": nrm(ks[4], (DEPTH, DEC_BATCH, CONV_W - 1, W_B), 1.0),
        "p_prompt": nrm(ks[5], (DEPTH, BATCH, SEQ, PLE_DIM), 1.0),
        "p_sample": nrm(ks[6], (DEPTH, DEC_BATCH, DEC_SEQ, PLE_DIM), 1.0),
        "g_mix": 1.0 + nrm(ks[7], (DEPTH, D_MODEL), 0.02),
        "w_in": nrm(ks[8], (DEPTH, D_MODEL, IN_COLS), D_MODEL ** -0.5),
        "lower_bounds": nrm(ks[9], (DEPTH + 1, W_A), 0.1),
        "g_hgrn_norm": 1.0 + nrm(ks[10], (DEPTH, W_A), 0.02),
        "conv_w": nrm(ks[11], (DEPTH, CONV_W, W_B), CONV_W ** -0.5),
        "conv_b": nrm(ks[12], (DEPTH, W_B), 0.01),
        "w_rg_a": nrm(ks[14], (DEPTH, N_BLK_B, BLK_W_B, BLK_W_B), BLK_W_B ** -0.5),
        "b_rg_a": nrm(ks[15], (DEPTH, W_B), 0.01),
        "w_rg_x": nrm(ks[16], (DEPTH, N_BLK_B, BLK_W_B, BLK_W_B), BLK_W_B ** -0.5),
        "b_rg_x": nrm(ks[17], (DEPTH, W_B), 0.01),
        "lambda_rg": lam,
        "w_out": nrm(ks[18], (DEPTH, MIX_W, D_MODEL), MIX_W ** -0.5),
        "g_ffn": 1.0 + nrm(ks[19], (DEPTH, D_MODEL), 0.02),
        "w_peer_q": nrm(ks[20], (DEPTH, D_MODEL, PEER_HEADS * PEER_DQ), D_MODEL ** -0.5),
        "peer_keys": nrm(ks[21], (DEPTH, PEER_HEADS, 2, N_SUBKEYS, PEER_DHALF), PEER_DHALF ** -0.5),
        "peer_u": nrm(ks[22], (DEPTH, N_EXPERTS, D_MODEL), D_MODEL ** -0.5),
        "peer_v": nrm(ks[23], (DEPTH, N_EXPERTS, D_MODEL), PEER_HEADS ** -0.5),
        "g_ple": 1.0 + nrm(ks[24], (DEPTH, D_MODEL), 0.02),
        "w_ple_gate": nrm(ks[25], (DEPTH, D_MODEL, D_MODEL), D_MODEL ** -0.5),
        "w_ple": nrm(ks[26], (DEPTH, PLE_DIM, D_MODEL), PLE_DIM ** -0.5),
        "g_final": 1.0 + nrm(ks[27], (D_MODEL,), 0.02),
    }


def reference(x_prompt, x_sample, state_hgrn, state_rglru, state_conv, p_prompt, p_sample,
              g_mix, w_in, lower_bounds, g_hgrn_norm, conv_w, conv_b, w_rg_a, b_rg_a, w_rg_x, b_rg_x,
              lambda_rg, w_out, g_ffn, w_peer_q, peer_keys, peer_u, peer_v, g_ple, w_ple_gate, w_ple, g_final):
    lb_all = jnp.cumsum(jax.nn.softmax(lower_bounds.astype(jnp.float32), axis=0), axis=0)

    h_p = x_prompt
    h_s = x_sample
    hg_p, rg_p, cv_p, hg_s, rg_s, cv_s = [], [], [], [], [], []
    for l in range(DEPTH):
        lp = dict(g_mix=g_mix[l], w_in=w_in[l], g_hgrn_norm=g_hgrn_norm[l], conv_w=conv_w[l], conv_b=conv_b[l],
                  w_rg_a=w_rg_a[l], b_rg_a=b_rg_a[l], w_rg_x=w_rg_x[l], b_rg_x=b_rg_x[l], lambda_rg=lambda_rg[l],
                  w_out=w_out[l], g_ffn=g_ffn[l], w_peer_q=w_peer_q[l], peer_keys=peer_keys[l],
                  peer_u=peer_u[l], peer_v=peer_v[l], g_ple=g_ple[l], w_ple_gate=w_ple_gate[l], w_ple=w_ple[l])
        lb = lb_all[l].reshape(H_A, DK)
        Bp = h_p.shape[0]
        h_p, S1, r1, c1 = run_layer(h_p, p_prompt[l], jnp.zeros((Bp, H_A, DK, DV), jnp.float32),
                                    jnp.zeros((Bp, W_B), jnp.float32),
                                    jnp.zeros((Bp, CONV_W - 1, W_B), h_p.dtype), lb, lp)
        h_s, S2, r2, c2 = run_layer(h_s, p_sample[l], state_hgrn[l], state_rglru[l], state_conv[l], lb, lp)
        hg_p.append(S1); rg_p.append(r1); cv_p.append(c1)
        hg_s.append(S2); rg_s.append(r2); cv_s.append(c2)

    y_prompt = rmsnorm(h_p, g_final)
    y_sample = rmsnorm(h_s, g_final)
    new_hgrn_prompt = jnp.stack(hg_p, axis=0)
    new_rglru_prompt = jnp.stack(rg_p, axis=0)
    new_conv_prompt = jnp.stack(cv_p, axis=0)
    new_hgrn_sample = jnp.stack(hg_s, axis=0)
    new_rglru_sample = jnp.stack(rg_s, axis=0)
    new_conv_sample = jnp.stack(cv_s, axis=0)
    return (y_prompt, y_sample, new_hgrn_prompt, new_rglru_prompt, new_conv_prompt,
            new_hgrn_sample, new_rglru_sample, new_conv_sample)
```

```python
import functools
import math

import jax
import jax.numpy as jnp
from jax import lax
from jax.experimental import pallas as pl
from jax.experimental.pallas import tpu as pltpu

D_MODEL = 1024
W_A = 512
DK = 128
DV = 128
H_A = W_A // DK
W_B = 512
CONV_W = 4
RG_C = 8.0
IN_COLS = 4 * W_A + 2 * W_B
CHUNK = 64
PEER_HEADS = 8
N_SUBKEYS = 128
PEER_TOPK = 16
PEER_DQ = 256
PEER_DHALF = PEER_DQ // 2
PEER_SEL = PEER_HEADS * PEER_TOPK
PLE_DIM = 256
EPS = 1e-6

LANES = 128
SUBLANES = 8
CONV_PAD = SUBLANES
MIB = 1024 * 1024

F32 = jnp.float32
BF16 = jnp.bfloat16


def _block(n, target):
    b = min(n, target)
    while n % b:
        b -= 1
    return b


def _params(semantics, vmem_mib):
    return pltpu.CompilerParams(dimension_semantics=semantics, vmem_limit_bytes=vmem_mib * MIB)


def _rms(x, g):
    ms = jnp.mean(x * x, axis=-1, keepdims=True)
    return x * lax.rsqrt(ms + EPS) * g


def _sigmoid(x):
    return 1.0 / (1.0 + jnp.exp(-x))


def _gelu_tanh(x):
    c = math.sqrt(2.0 / math.pi)
    return x * (0.5 * (1.0 + jnp.tanh(c * (x + 0.044715 * (x * x * x)))))


def _bdot(a, b):
    return jnp.dot(a.astype(BF16), b.astype(BF16), preferred_element_type=F32)


def _bdot_nt(a, b):
    return lax.dot_general(a.astype(BF16), b.astype(BF16), (((1,), (1,)), ((), ())),
                           preferred_element_type=F32)


def _bdot_tn(a, b):
    return lax.dot_general(a.astype(BF16), b.astype(BF16), (((0,), (0,)), ((), ())),
                           preferred_element_type=F32)


def _cumsum_rows(x):
    n = x.shape[0]
    rows = lax.broadcasted_iota(jnp.int32, x.shape, 0)
    s = 1
    while s < n:
        x = x + jnp.where(rows >= s, pltpu.roll(x, s, 0), 0.0)
        s *= 2
    return x


def _linear_scan_rows(a, b):
    n = a.shape[0]
    rows = lax.broadcasted_iota(jnp.int32, a.shape, 0)
    s = 1
    while s < n:
        keep = rows >= s
        b = jnp.where(keep, a * pltpu.roll(b, s, 0) + b, b)
        a = jnp.where(keep, a * pltpu.roll(a, s, 0), a)
        s *= 2
    return a, b


def _in_proj_kernel(h_ref, g_ref, w_ref, z_ref):
    u = _rms(h_ref[...], g_ref[...])
    z_ref[...] = _bdot(u, w_ref[...])


def _in_proj(h2d, g_mix, w_in):
    T = h2d.shape[0]
    tb = _block(T, 256)
    return pl.pallas_call(
        _in_proj_kernel,
        out_shape=jax.ShapeDtypeStruct((T, IN_COLS), F32),
        grid=(T // tb,),
        in_specs=[pl.BlockSpec((tb, D_MODEL), lambda i: (i, 0)),
                  pl.BlockSpec((1, D_MODEL), lambda i: (0, 0)),
                  pl.BlockSpec((D_MODEL, IN_COLS), lambda i: (0, 0))],
        out_specs=pl.BlockSpec((tb, IN_COLS), lambda i: (i, 0)),
        compiler_params=_params(("arbitrary",), 40),
        name="in_proj",
    )(h2d, g_mix.reshape(1, D_MODEL), w_in.astype(BF16))


def _hgrn_kernel(zq_ref, zf_ref, zi_ref, zg_ref, lbw_ref, gn_ref, s0_ref,
                 o_ref, s_out_ref, st_scr, *, chunk, n_chunks, layer):
    t = pl.program_id(1)

    @pl.when(t == 0)
    def _():
        for h in range(H_A):
            st_scr[h] = s0_ref[0, h].T

    lbw = lbw_ref[...]
    ex = jnp.exp(lbw - jnp.max(lbw, axis=0, keepdims=True))
    sm = ex / jnp.sum(ex, axis=0, keepdims=True)
    lb = jnp.sum(sm[0:layer + 1, :], axis=0, keepdims=True)
    gn = gn_ref[...]

    tril = (lax.broadcasted_iota(jnp.int32, (chunk, chunk), 0)
            >= lax.broadcasted_iota(jnp.int32, (chunk, chunk), 1))

    for c in range(n_chunks):
        rows = pl.ds(c * chunk, chunk)
        zq = zq_ref[0, rows, :]
        zf = zf_ref[0, rows, :]
        v = zi_ref[0, rows, :]
        zg = zg_ref[0, rows, :]
        f = lb + (1.0 - lb) * _sigmoid(zf)
        k = 1.0 - f
        q = zq * _sigmoid(zq)
        cum = _cumsum_rows(jnp.log(f))
        qd = q * jnp.exp(cum)
        kd = k * jnp.exp(-cum)
        last = cum[chunk - 1:chunk, :]
        kdec = k * jnp.exp(last - cum)
        dec = jnp.exp(last)
        for h in range(H_A):
            sl = slice(h * DK, (h + 1) * DK)
            att = jnp.where(tril, _bdot_nt(qd[:, sl], kd[:, sl]), 0.0)
            st = st_scr[h]
            o = _bdot(att, v[:, sl]) + _bdot_nt(qd[:, sl], st)
            st_scr[h] = st * dec[:, sl] + _bdot_tn(v[:, sl], kdec[:, sl])
            on = _rms(o, gn[:, sl])
            zgh = zg[:, sl]
            o_ref[0, rows, sl] = on * (zgh * _sigmoid(zgh))

    @pl.when(t == pl.num_programs(1) - 1)
    def _():
        for h in range(H_A):
            s_out_ref[0, h] = st_scr[h].T


def _hgrn(z3, lower_bounds, g_norm, s0, layer):
    B, L, _ = z3.shape
    chunk = min(CHUNK, L)
    n_chunks = _block(L // chunk, 4)
    tb = chunk * n_chunks
    zspec = lambda j: pl.BlockSpec((1, tb, W_A), lambda b, t, j=j: (b, t, j))
    kern = functools.partial(_hgrn_kernel, chunk=chunk, n_chunks=n_chunks, layer=layer)
    return pl.pallas_call(
        kern,
        out_shape=(jax.ShapeDtypeStruct((B, L, W_A), F32),
                   jax.ShapeDtypeStruct((B, H_A, DK, DV), F32)),
        grid=(B, L // tb),
        in_specs=[zspec(0), zspec(1), zspec(2), zspec(3),
                  pl.BlockSpec(lower_bounds.shape, lambda b, t: (0, 0)),
                  pl.BlockSpec((1, W_A), lambda b, t: (0, 0)),
                  pl.BlockSpec((1, H_A, DK, DV), lambda b, t: (b, 0, 0, 0))],
        out_specs=(pl.BlockSpec((1, tb, W_A), lambda b, t: (b, t, 0)),
                   pl.BlockSpec((1, H_A, DK, DV), lambda b, t: (b, 0, 0, 0))),
        scratch_shapes=[pltpu.VMEM((H_A, DV, DK), F32)],
        compiler_params=_params(("arbitrary", "arbitrary"), 32),
        name="hgrn",
    )(z3, z3, z3, z3, lower_bounds, g_norm.reshape(1, W_A), s0)


def _rglru_kernel(zx_ref, zy_ref, cw_ref, cb_ref, wa_ref, ba_ref, wx_ref, bx_ref, lam_ref,
                  buf0_ref, h0_ref, o_ref, hlast_ref, bufnew_ref, xp_scr, hc_scr, *, tb):
    t = pl.program_id(1)
    hist = CONV_W - 1

    @pl.when(t == 0)
    def _():
        xp_scr[CONV_PAD - hist:CONV_PAD, :] = buf0_ref[0]
        hc_scr[...] = h0_ref[0]

    x = zx_ref[0]
    xp_scr[CONV_PAD:CONV_PAD + tb, :] = x
    cw = cw_ref[...]
    acc = xp_scr[CONV_PAD - hist:CONV_PAD - hist + tb, :] * cw[0:1, :]
    for j in range(1, CONV_W):
        acc = acc + xp_scr[CONV_PAD - hist + j:CONV_PAD - hist + j + tb, :] * cw[j:j + 1, :]
    xc = cb_ref[...] + acc
    tail = xp_scr[CONV_PAD + tb - hist:CONV_PAD + tb, :]
    xp_scr[CONV_PAD - hist:CONV_PAD, :] = tail

    r = _sigmoid(_bdot(xc, wa_ref[...]) + ba_ref[...])
    gi = _sigmoid(_bdot(xc, wx_ref[...]) + bx_ref[...])
    nl = -lam_ref[...]
    softplus = jnp.maximum(nl, 0.0) + jnp.log1p(jnp.exp(-jnp.abs(nl)))
    log_a = -RG_C * r * softplus
    a = jnp.exp(log_a)
    one_minus_a2 = -jnp.tanh(log_a) * (a * a + 1.0)
    b = jnp.sqrt(one_minus_a2) * (gi * xc)
    aa, bb = _linear_scan_rows(a, b)
    hseq = bb + aa * hc_scr[...]
    hc_scr[...] = hseq[tb - 1:tb, :]
    o_ref[0] = hseq * _gelu_tanh(zy_ref[0])

    @pl.when(t == pl.num_programs(1) - 1)
    def _():
        hlast_ref[0] = hseq[tb - 1:tb, :]
        bufnew_ref[0] = tail


def _block_diag(w):
    n, c, d = w.shape
    eye = jnp.eye(n, dtype=w.dtype)
    return (eye[:, None, :, None] * w[:, :, None, :]).reshape(n * c, n * d)


def _rglru(z3, conv_w, conv_b, w_a, b_a, w_x, b_x, lam, buf0, h0):
    B, L, _ = z3.shape
    tb = _block(L, 256)
    row = lambda v: v.reshape(1, W_B)
    full = lambda shape: pl.BlockSpec(shape, lambda b, t: (0,) * len(shape))
    kern = functools.partial(_rglru_kernel, tb=tb)
    return pl.pallas_call(
        kern,
        out_shape=(jax.ShapeDtypeStruct((B, L, W_B), F32),
                   jax.ShapeDtypeStruct((B, 1, W_B), F32),
                   jax.ShapeDtypeStruct((B, CONV_W - 1, W_B), F32)),
        grid=(B, L // tb),
        in_specs=[pl.BlockSpec((1, tb, W_B), lambda b, t: (b, t, 4)),
                  pl.BlockSpec((1, tb, W_B), lambda b, t: (b, t, 5)),
                  full((CONV_W, W_B)), full((1, W_B)),
                  full((W_B, W_B)), full((1, W_B)),
                  full((W_B, W_B)), full((1, W_B)), full((1, W_B)),
                  pl.BlockSpec((1, CONV_W - 1, W_B), lambda b, t: (b, 0, 0)),
                  pl.BlockSpec((1, 1, W_B), lambda b, t: (b, 0, 0))],
        out_specs=(pl.BlockSpec((1, tb, W_B), lambda b, t: (b, t, 0)),
                   pl.BlockSpec((1, 1, W_B), lambda b, t: (b, 0, 0)),
                   pl.BlockSpec((1, CONV_W - 1, W_B), lambda b, t: (b, 0, 0))),
        scratch_shapes=[pltpu.VMEM((CONV_PAD + tb, W_B), F32), pltpu.VMEM((1, W_B), F32)],
        compiler_params=_params(("arbitrary", "arbitrary"), 32),
        name="rglru",
    )(z3, z3, conv_w, row(conv_b), _block_diag(w_a).astype(BF16), row(b_a),
      _block_diag(w_x).astype(BF16), row(b_x), row(lam), buf0, h0.reshape(B, 1, W_B))


def _out_proj_kernel(h_ref, oa_ref, ob_ref, wt_ref, wb_ref, g_ref, wq_ref, h1_ref, xn_ref, q_ref):
    h1 = h_ref[...] + (_bdot(oa_ref[...], wt_ref[...]) + _bdot(ob_ref[...], wb_ref[...]))
    h1_ref[...] = h1
    xn = _rms(h1, g_ref[...])
    xn_ref[...] = xn
    q_ref[...] = _bdot(xn, wq_ref[...])


def _out_proj(h2d, oa, ob, w_out, g_ffn, w_q):
    T = h2d.shape[0]
    tb = _block(T, 256)
    nq = PEER_HEADS * PEER_DQ
    w_out = w_out.astype(BF16)
    tok = lambda w: pl.BlockSpec((tb, w), lambda i: (i, 0))
    full = lambda shape: pl.BlockSpec(shape, lambda i: (0, 0))
    return pl.pallas_call(
        _out_proj_kernel,
        out_shape=(jax.ShapeDtypeStruct((T, D_MODEL), F32),
                   jax.ShapeDtypeStruct((T, D_MODEL), F32),
                   jax.ShapeDtypeStruct((T, nq), F32)),
        grid=(T // tb,),
        in_specs=[tok(D_MODEL), tok(W_A), tok(W_B),
                  full((W_A, D_MODEL)), full((W_B, D_MODEL)), full((1, D_MODEL)),
                  full((D_MODEL, nq))],
        out_specs=(tok(D_MODEL), tok(D_MODEL), tok(nq)),
        compiler_params=_params(("arbitrary",), 48),
        name="out_proj",
    )(h2d, oa, ob, w_out[:W_A], w_out[W_A:], g_ffn.reshape(1, D_MODEL), w_q.astype(BF16))


def _topk_rows(vals, k, payload=None):
    n, w = vals.shape
    rows = lax.broadcasted_iota(jnp.int32, (n, w), 0).astype(F32)
    slot = lax.broadcasted_iota(jnp.int32, (k, w), 0)
    out_v = jnp.zeros((k, w), F32)
    out_p = jnp.zeros((k, w), F32)
    for i in range(k):
        m = jnp.max(vals, axis=0, keepdims=True)
        hit = jnp.min(jnp.where(vals == m, rows, float(n)), axis=0, keepdims=True)
        sel = rows == hit
        if payload is None:
            pick = hit
        else:
            pick = jnp.max(jnp.where(sel, payload, -1.0), axis=0, keepdims=True)
        out_v = jnp.where(slot == i, m, out_v)
        out_p = jnp.where(slot == i, pick, out_p)
        vals = jnp.where(sel, -jnp.inf, vals)
    return out_v, out_p


def _route_kernel(q_ref, keys_ref, idx_ref, gate_ref):
    def head(h, carry):
        col = pl.multiple_of(h * PEER_DQ, PEER_DQ)
        svs, sis = [], []
        for p in range(2):
            qs = q_ref[:, pl.ds(col + p * PEER_DHALF, PEER_DHALF)]
            s = _bdot_nt(keys_ref[h, p], qs)
            sv, si = _topk_rows(s, PEER_TOPK)
            svs.append(sv)
            sis.append(si)
        cand = jnp.concatenate([svs[0][i:i + 1, :] + svs[1] for i in range(PEER_TOPK)], axis=0)
        cidx = jnp.concatenate([sis[0][i:i + 1, :] * float(N_SUBKEYS) + sis[1]
                                for i in range(PEER_TOPK)], axis=0)
        cs, ce = _topk_rows(cand, PEER_TOPK, payload=cidx)
        ex = jnp.exp(cs - cs[0:1, :])
        g = ex / jnp.sum(ex, axis=0, keepdims=True)
        row = pl.multiple_of(h * PEER_TOPK, PEER_TOPK)
        idx_ref[pl.ds(row, PEER_TOPK), :] = ce.astype(jnp.int32)
        gate_ref[pl.ds(row, PEER_TOPK), :] = g
        return carry

    lax.fori_loop(0, PEER_HEADS, head, 0)


def _route(q, keys):
    T = q.shape[0]
    tb = _block(T, 256)
    nq = PEER_HEADS * PEER_DQ
    return pl.pallas_call(
        _route_kernel,
        out_shape=(jax.ShapeDtypeStruct((PEER_SEL, T), jnp.int32),
                   jax.ShapeDtypeStruct((PEER_SEL, T), F32)),
        grid=(T // tb,),
        in_specs=[pl.BlockSpec((tb, nq), lambda i: (i, 0)),
                  pl.BlockSpec(keys.shape, lambda i: (0, 0, 0, 0))],
        out_specs=(pl.BlockSpec((PEER_SEL, tb), lambda i: (0, i)),
                   pl.BlockSpec((PEER_SEL, tb), lambda i: (0, i))),
        compiler_params=_params(("arbitrary",), 32),
        name="route",
    )(q, keys.astype(BF16))


GATHER_TOKENS = 32
GATHER_SLOTS = 2


def _gather_kernel(idx_ref, xn_ref, h1_ref, gate_ref, u_hbm, v_hbm, out_ref,
                   ubuf, vbuf, sem, *, tbg, steps_per_gate_block):
    lane0 = (pl.program_id(0) % steps_per_gate_block) * tbg

    def row_copies(tok, slot):
        for k in range(PEER_SEL):
            e = idx_ref[0, 0, tok * PEER_SEL + k]
            yield pltpu.make_async_copy(u_hbm.at[pl.ds(e, 1), :], ubuf.at[slot, pl.ds(k, 1), :],
                                        sem.at[slot])
            yield pltpu.make_async_copy(v_hbm.at[pl.ds(e, 1), :], vbuf.at[slot, pl.ds(k, 1), :],
                                        sem.at[slot])

    def issue(tok, slot):
        for cp in row_copies(tok, slot):
            cp.start()

    def wait(slot):
        pltpu.make_async_copy(ubuf.at[slot], ubuf.at[slot], sem.at[slot]).wait()
        pltpu.make_async_copy(vbuf.at[slot], vbuf.at[slot], sem.at[slot]).wait()

    issue(0, 0)
    lanes = lax.broadcasted_iota(jnp.int32, gate_ref.shape, 1)

    def body(tok, carry):
        slot = tok % GATHER_SLOTS

        @pl.when(tok + 1 < tbg)
        def _():
            issue(tok + 1, (tok + 1) % GATHER_SLOTS)

        wait(slot)
        x = xn_ref[pl.ds(tok, 1), :]
        act = jnp.sum(ubuf[slot] * x, axis=1, keepdims=True)
        g = jnp.sum(jnp.where(lanes == lane0 + tok, gate_ref[...], 0.0), axis=1, keepdims=True)
        w = g * _gelu_tanh(act)
        o = jnp.sum(vbuf[slot] * w, axis=0, keepdims=True)
        out_ref[pl.ds(tok, 1), :] = h1_ref[pl.ds(tok, 1), :] + o
        return carry

    lax.fori_loop(0, tbg, body, 0)


def _gather(eidx, gates, xn, h1, peer_u, peer_v):
    T = xn.shape[0]
    tbg = _block(T, GATHER_TOKENS)
    gate_tb = _block(T, LANES)
    per = gate_tb // tbg
    n_exp = peer_u.shape[0]
    idx = eidx.T.reshape(T // tbg, 1, tbg * PEER_SEL)
    kern = functools.partial(_gather_kernel, tbg=tbg, steps_per_gate_block=per)
    return pl.pallas_call(
        kern,
        out_shape=jax.ShapeDtypeStruct((T, D_MODEL), F32),
        grid=(T // tbg,),
        in_specs=[pl.BlockSpec((1, 1, tbg * PEER_SEL), lambda i: (i, 0, 0),
                               memory_space=pltpu.SMEM),
                  pl.BlockSpec((tbg, D_MODEL), lambda i: (i, 0)),
                  pl.BlockSpec((tbg, D_MODEL), lambda i: (i, 0)),
                  pl.BlockSpec((PEER_SEL, gate_tb), lambda i: (0, i // per)),
                  pl.BlockSpec(memory_space=pl.ANY),
                  pl.BlockSpec(memory_space=pl.ANY)],
        out_specs=pl.BlockSpec((tbg, D_MODEL), lambda i: (i, 0)),
        scratch_shapes=[pltpu.VMEM((GATHER_SLOTS, PEER_SEL, D_MODEL), F32),
                        pltpu.VMEM((GATHER_SLOTS, PEER_SEL, D_MODEL), F32),
                        pltpu.SemaphoreType.DMA((GATHER_SLOTS,))],
        compiler_params=_params(("arbitrary",), 32),
        name="gather",
    )(idx, xn, h1, gates, peer_u.reshape(n_exp, D_MODEL), peer_v.reshape(n_exp, D_MODEL))


def _ple_kernel(h_ref, p_ref, wp_ref, gp_ref, wg_ref, gf_ref, y_ref):
    h2 = h_ref[...]
    gate = _sigmoid(_bdot(_rms(h2, gp_ref[...]), wg_ref[...]))
    h3 = h2 + _bdot(p_ref[...], wp_ref[...]) * gate
    y_ref[...] = _rms(h3, gf_ref[...])


def _ple(h2, p2d, w_ple, g_ple, w_gate, g_final):
    T = h2.shape[0]
    tb = _block(T, 256)
    tok = lambda w: pl.BlockSpec((tb, w), lambda i: (i, 0))
    full = lambda shape: pl.BlockSpec(shape, lambda i: (0, 0))
    return pl.pallas_call(
        _ple_kernel,
        out_shape=jax.ShapeDtypeStruct((T, D_MODEL), F32),
        grid=(T // tb,),
        in_specs=[tok(D_MODEL), tok(PLE_DIM), full((PLE_DIM, D_MODEL)), full((1, D_MODEL)),
                  full((D_MODEL, D_MODEL)), full((1, D_MODEL))],
        out_specs=tok(D_MODEL),
        compiler_params=_params(("arbitrary",), 32),
        name="ple",
    )(h2, p2d, w_ple.astype(BF16), g_ple.reshape(1, D_MODEL), w_gate.astype(BF16),
      g_final.reshape(1, D_MODEL))


def _run_stream(h, p, s0, r0, c0, lw, g_final, lower_bounds, layer):
    B, L, _ = h.shape
    T = B * L
    h2d = h.reshape(T, D_MODEL)
    z = _in_proj(h2d, lw["g_mix"], lw["w_in"])
    z3 = z.reshape(B, L, IN_COLS)
    oa, s_new = _hgrn(z3, lower_bounds, lw["g_hgrn_norm"], s0, layer)
    ob, r_new, c_new = _rglru(z3, lw["conv_w"], lw["conv_b"], lw["w_rg_a"], lw["b_rg_a"],
                              lw["w_rg_x"], lw["b_rg_x"], lw["lambda_rg"], c0, r0)
    h1, xn, q = _out_proj(h2d, oa.reshape(T, W_A), ob.reshape(T, W_B), lw["w_out"],
                          lw["g_ffn"], lw["w_peer_q"])
    eidx, gates = _route(q, lw["peer_keys"])
    h2 = _gather(eidx, gates, xn, h1, lw["peer_u"], lw["peer_v"])
    y = _ple(h2, p.reshape(T, PLE_DIM), lw["w_ple"], lw["g_ple"], lw["w_ple_gate"], g_final)
    return y.reshape(B, L, D_MODEL), s_new, r_new.reshape(B, W_B), c_new


def kernel(x_prompt, x_sample, state_hgrn, state_rglru, state_conv, p_prompt, p_sample, g_mix, w_in, lower_bounds, g_hgrn_norm, conv_w, conv_b, w_rg_a, b_rg_a, w_rg_x, b_rg_x, lambda_rg, w_out, g_ffn, w_peer_q, peer_keys, peer_u, peer_v, g_ple, w_ple_gate, w_ple, g_final):
    depth = w_in.shape[0]
    assert depth == 1, "the final norm is fused into the last layer's kernel; depth 1 only"
    layer = 0
    lw = dict(g_mix=g_mix[layer], w_in=w_in[layer], g_hgrn_norm=g_hgrn_norm[layer],
              conv_w=conv_w[layer], conv_b=conv_b[layer], w_rg_a=w_rg_a[layer], b_rg_a=b_rg_a[layer],
              w_rg_x=w_rg_x[layer], b_rg_x=b_rg_x[layer], lambda_rg=lambda_rg[layer],
              w_out=w_out[layer], g_ffn=g_ffn[layer], w_peer_q=w_peer_q[layer],
              peer_keys=peer_keys[layer], peer_u=peer_u[layer], peer_v=peer_v[layer],
              g_ple=g_ple[layer], w_ple_gate=w_ple_gate[layer], w_ple=w_ple[layer])
    bp = x_prompt.shape[0]
    y_p, s_p, r_p, c_p = _run_stream(
        x_prompt, p_prompt[layer], jnp.zeros((bp, H_A, DK, DV), F32), jnp.zeros((bp, W_B), F32),
        jnp.zeros((bp, CONV_W - 1, W_B), F32), lw, g_final, lower_bounds, layer)
    y_s, s_s, r_s, c_s = _run_stream(
        x_sample, p_sample[layer], state_hgrn[layer], state_rglru[layer], state_conv[layer],
        lw, g_final, lower_bounds, layer)
    return (y_p, y_s, s_p[None], r_p[None], c_p[None], s_s[None], r_s[None], c_s[None])
```

```python
import functools
import math

import jax
import jax.numpy as jnp
from jax import lax
from jax.experimental import pallas as pl
from jax.experimental.pallas import tpu as pltpu

D_MODEL = 1024
W_A = 512
DK = 128
DV = 128
H_A = W_A // DK
W_B = 512
CONV_W = 4
RG_C = 8.0
IN_COLS = 4 * W_A + 2 * W_B
CHUNK = 64
PEER_HEADS = 8
N_SUBKEYS = 128
PEER_TOPK = 16
PEER_DQ = 256
PEER_DHALF = PEER_DQ // 2
PEER_SEL = PEER_HEADS * PEER_TOPK
PLE_DIM = 256
EPS = 1e-6

LANES = 128
SUBLANES = 8
CONV_PAD = SUBLANES
MIB = 1024 * 1024

F32 = jnp.float32
BF16 = jnp.bfloat16


def _block(n, target):
    b = min(n, target)
    while n % b:
        b -= 1
    return b


def _params(semantics, vmem_mib):
    return pltpu.CompilerParams(dimension_semantics=semantics, vmem_limit_bytes=vmem_mib * MIB)


def _rms(x, g):
    ms = jnp.mean(x * x, axis=-1, keepdims=True)
    return x * lax.rsqrt(ms + EPS) * g


def _sigmoid(x):
    return 1.0 / (1.0 + jnp.exp(-x))


def _gelu_tanh(x):
    c = math.sqrt(2.0 / math.pi)
    return x * (0.5 * (1.0 + jnp.tanh(c * (x + 0.044715 * (x * x * x)))))


def _bdot(a, b):
    return jnp.dot(a.astype(BF16), b.astype(BF16), preferred_element_type=F32)


def _bdot_nt(a, b):
    return lax.dot_general(a.astype(BF16), b.astype(BF16), (((1,), (1,)), ((), ())),
                           preferred_element_type=F32)


def _bdot_tn(a, b):
    return lax.dot_general(a.astype(BF16), b.astype(BF16), (((0,), (0,)), ((), ())),
                           preferred_element_type=F32)


def _cumsum_rows(x):
    n = x.shape[0]
    rows = lax.broadcasted_iota(jnp.int32, x.shape, 0)
    s = 1
    while s < n:
        x = x + jnp.where(rows >= s, pltpu.roll(x, s, 0), 0.0)
        s *= 2
    return x


def _linear_scan_rows(a, b):
    n = a.shape[0]
    rows = lax.broadcasted_iota(jnp.int32, a.shape, 0)
    s = 1
    while s < n:
        keep = rows >= s
        b = jnp.where(keep, a * pltpu.roll(b, s, 0) + b, b)
        a = jnp.where(keep, a * pltpu.roll(a, s, 0), a)
        s *= 2
    return a, b


def _in_proj_kernel(h_ref, g_ref, w_ref, z_ref):
    u = _rms(h_ref[...], g_ref[...])
    z_ref[...] = _bdot(u, w_ref[...])


def _in_proj(h2d, g_mix, w_in):
    T = h2d.shape[0]
    tb = _block(T, 256)
    return pl.pallas_call(
        _in_proj_kernel,
        out_shape=jax.ShapeDtypeStruct((T, IN_COLS), F32),
        grid=(T // tb,),
        in_specs=[pl.BlockSpec((tb, D_MODEL), lambda i: (i, 0)),
                  pl.BlockSpec((1, D_MODEL), lambda i: (0, 0)),
                  pl.BlockSpec((D_MODEL, IN_COLS), lambda i: (0, 0))],
        out_specs=pl.BlockSpec((tb, IN_COLS), lambda i: (i, 0)),
        compiler_params=_params(("arbitrary",), 40),
        name="in_proj",
    )(h2d, g_mix.reshape(1, D_MODEL), w_in.astype(BF16))


def _hgrn_kernel(zq_ref, zf_ref, zi_ref, zg_ref, lbw_ref, gn_ref, s0_ref,
                 o_ref, s_out_ref, st_scr, *, chunk, n_chunks, layer):
    t = pl.program_id(1)

    @pl.when(t == 0)
    def _():
        for h in range(H_A):
            st_scr[h] = s0_ref[0, h].T

    lbw = lbw_ref[...]
    ex = jnp.exp(lbw - jnp.max(lbw, axis=0, keepdims=True))
    sm = ex / jnp.sum(ex, axis=0, keepdims=True)
    lb = jnp.sum(sm[0:layer + 1, :], axis=0, keepdims=True)
    gn = gn_ref[...]

    tril = (lax.broadcasted_iota(jnp.int32, (chunk, chunk), 0)
            >= lax.broadcasted_iota(jnp.int32, (chunk, chunk), 1))

    for c in range(n_chunks):
        rows = pl.ds(c * chunk, chunk)
        zq = zq_ref[0, rows, :]
        zf = zf_ref[0, rows, :]
        v = zi_ref[0, rows, :]
        zg = zg_ref[0, rows, :]
        f = lb + (1.0 - lb) * _sigmoid(zf)
        k = 1.0 - f
        q = zq * _sigmoid(zq)
        cum = _cumsum_rows(jnp.log(f))
        qd = q * jnp.exp(cum)
        kd = k * jnp.exp(-cum)
        last = cum[chunk - 1:chunk, :]
        kdec = k * jnp.exp(last - cum)
        dec = jnp.exp(last)
        for h in range(H_A):
            sl = slice(h * DK, (h + 1) * DK)
            att = jnp.where(tril, _bdot_nt(qd[:, sl], kd[:, sl]), 0.0)
            st = st_scr[h]
            o = _bdot(att, v[:, sl]) + _bdot_nt(qd[:, sl], st)
            st_scr[h] = st * dec[:, sl] + _bdot_tn(v[:, sl], kdec[:, sl])
            on = _rms(o, gn[:, sl])
            zgh = zg[:, sl]
            o_ref[0, rows, sl] = on * (zgh * _sigmoid(zgh))

    @pl.when(t == pl.num_programs(1) - 1)
    def _():
        for h in range(H_A):
            s_out_ref[0, h] = st_scr[h].T


def _hgrn(z3, lower_bounds, g_norm, s0, layer):
    B, L, _ = z3.shape
    chunk = min(CHUNK, L)
    n_chunks = _block(L // chunk, 4)
    tb = chunk * n_chunks
    zspec = lambda j: pl.BlockSpec((1, tb, W_A), lambda b, t, j=j: (b, t, j))
    kern = functools.partial(_hgrn_kernel, chunk=chunk, n_chunks=n_chunks, layer=layer)
    return pl.pallas_call(
        kern,
        out_shape=(jax.ShapeDtypeStruct((B, L, W_A), F32),
                   jax.ShapeDtypeStruct((B, H_A, DK, DV), F32)),
        grid=(B, L // tb),
        in_specs=[zspec(0), zspec(1), zspec(2), zspec(3),
                  pl.BlockSpec(lower_bounds.shape, lambda b, t: (0, 0)),
                  pl.BlockSpec((1, W_A), lambda b, t: (0, 0)),
                  pl.BlockSpec((1, H_A, DK, DV), lambda b, t: (b, 0, 0, 0))],
        out_specs=(pl.BlockSpec((1, tb, W_A), lambda b, t: (b, t, 0)),
                   pl.BlockSpec((1, H_A, DK, DV), lambda b, t: (b, 0, 0, 0))),
        scratch_shapes=[pltpu.VMEM((H_A, DV, DK), F32)],
        compiler_params=_params(("arbitrary", "arbitrary"), 32),
        name="hgrn",
    )(z3, z3, z3, z3, lower_bounds, g_norm.reshape(1, W_A), s0)


def _rglru_kernel(zx_ref, zy_ref, cw_ref, cb_ref, wa_ref, ba_ref, wx_ref, bx_ref, lam_ref,
                  buf0_ref, h0_ref, o_ref, hlast_ref, bufnew_ref, xp_scr, hc_scr, *, tb):
    t = pl.program_id(1)
    hist = CONV_W - 1

    @pl.when(t == 0)
    def _():
        xp_scr[CONV_PAD - hist:CONV_PAD, :] = buf0_ref[0]
        hc_scr[...] = h0_ref[0]

    x = zx_ref[0]
    xp_scr[CONV_PAD:CONV_PAD + tb, :] = x
    cw = cw_ref[...]
    acc = xp_scr[CONV_PAD - hist:CONV_PAD - hist + tb, :] * cw[0:1, :]
    for j in range(1, CONV_W):
        acc = acc + xp_scr[CONV_PAD - hist + j:CONV_PAD - hist + j + tb, :] * cw[j:j + 1, :]
    xc = cb_ref[...] + acc
    tail = xp_scr[CONV_PAD + tb - hist:CONV_PAD + tb, :]
    xp_scr[CONV_PAD - hist:CONV_PAD, :] = tail

    r = _sigmoid(_bdot(xc, wa_ref[...]) + ba_ref[...])
    gi = _sigmoid(_bdot(xc, wx_ref[...]) + bx_ref[...])
    nl = -lam_ref[...]
    softplus = jnp.maximum(nl, 0.0) + jnp.log1p(jnp.exp(-jnp.abs(nl)))
    log_a = -RG_C * r * softplus
    a = jnp.exp(log_a)
    one_minus_a2 = -jnp.tanh(log_a) * (a * a + 1.0)
    b = jnp.sqrt(one_minus_a2) * (gi * xc)
    aa, bb = _linear_scan_rows(a, b)
    hseq = bb + aa * hc_scr[...]
    hc_scr[...] = hseq[tb - 1:tb, :]
    o_ref[0] = hseq * _gelu_tanh(zy_ref[0])

    @pl.when(t == pl.num_programs(1) - 1)
    def _():
        hlast_ref[0] = hseq[tb - 1:tb, :]
        bufnew_ref[0] = tail


def _block_diag(w):
    n, c, d = w.shape
    eye = jnp.eye(n, dtype=w.dtype)
    return (eye[:, None, :, None] * w[:, :, None, :]).reshape(n * c, n * d)


def _rglru(z3, conv_w, conv_b, w_a, b_a, w_x, b_x, lam, buf0, h0):
    B, L, _ = z3.shape
    tb = _block(L, 256)
    row = lambda v: v.reshape(1, W_B)
    full = lambda shape: pl.BlockSpec(shape, lambda b, t: (0,) * len(shape))
    kern = functools.partial(_rglru_kernel, tb=tb)
    return pl.pallas_call(
        kern,
        out_shape=(jax.ShapeDtypeStruct((B, L, W_B), F32),
                   jax.ShapeDtypeStruct((B, 1, W_B), F32),
                   jax.ShapeDtypeStruct((B, CONV_W - 1, W_B), F32)),
        grid=(B, L // tb),
        in_specs=[pl.BlockSpec((1, tb, W_B), lambda b, t: (b, t, 4)),
                  pl.BlockSpec((1, tb, W_B), lambda b, t: (b, t, 5)),
                  full((CONV_W, W_B)), full((1, W_B)),
                  full((W_B, W_B)), full((1, W_B)),
                  full((W_B, W_B)), full((1, W_B)), full((1, W_B)),
                  pl.BlockSpec((1, CONV_W - 1, W_B), lambda b, t: (b, 0, 0)),
                  pl.BlockSpec((1, 1, W_B), lambda b, t: (b, 0, 0))],
        out_specs=(pl.BlockSpec((1, tb, W_B), lambda b, t: (b, t, 0)),
                   pl.BlockSpec((1, 1, W_B), lambda b, t: (b, 0, 0)),
                   pl.BlockSpec((1, CONV_W - 1, W_B), lambda b, t: (b, 0, 0))),
        scratch_shapes=[pltpu.VMEM((CONV_PAD + tb, W_B), F32), pltpu.VMEM((1, W_B), F32)],
        compiler_params=_params(("arbitrary", "arbitrary"), 32),
        name="rglru",
    )(z3, z3, conv_w, row(conv_b), _block_diag(w_a).astype(BF16), row(b_a),
      _block_diag(w_x).astype(BF16), row(b_x), row(lam), buf0, h0.reshape(B, 1, W_B))


def _out_proj_kernel(h_ref, oa_ref, ob_ref, wt_ref, wb_ref, g_ref, wq_ref, h1_ref, xn_ref, q_ref):
    h1 = h_ref[...] + (_bdot(oa_ref[...], wt_ref[...]) + _bdot(ob_ref[...], wb_ref[...]))
    h1_ref[...] = h1
    xn = _rms(h1, g_ref[...])
    xn_ref[...] = xn
    q_ref[...] = _bdot(xn, wq_ref[...])


def _out_proj(h2d, oa, ob, w_out, g_ffn, w_q):
    T = h2d.shape[0]
    tb = _block(T, 256)
    nq = PEER_HEADS * PEER_DQ
    w_out = w_out.astype(BF16)
    tok = lambda w: pl.BlockSpec((tb, w), lambda i: (i, 0))
    full = lambda shape: pl.BlockSpec(shape, lambda i: (0, 0))
    return pl.pallas_call(
        _out_proj_kernel,
        out_shape=(jax.ShapeDtypeStruct((T, D_MODEL), F32),
                   jax.ShapeDtypeStruct((T, D_MODEL), F32),
                   jax.ShapeDtypeStruct((T, nq), F32)),
        grid=(T // tb,),
        in_specs=[tok(D_MODEL), tok(W_A), tok(W_B),
                  full((W_A, D_MODEL)), full((W_B, D_MODEL)), full((1, D_MODEL)),
                  full((D_MODEL, nq))],
        out_specs=(tok(D_MODEL), tok(D_MODEL), tok(nq)),
        compiler_params=_params(("arbitrary",), 48),
        name="out_proj",
    )(h2d, oa, ob, w_out[:W_A], w_out[W_A:], g_ffn.reshape(1, D_MODEL), w_q.astype(BF16))


def _topk_rows(vals, k, payload=None):
    n, w = vals.shape
    rows = lax.broadcasted_iota(jnp.int32, (n, w), 0).astype(F32)
    slot = lax.broadcasted_iota(jnp.int32, (k, w), 0)
    out_v = jnp.zeros((k, w), F32)
    out_p = jnp.zeros((k, w), F32)
    for i in range(k):
        m = jnp.max(vals, axis=0, keepdims=True)
        hit = jnp.min(jnp.where(vals == m, rows, float(n)), axis=0, keepdims=True)
        sel = rows == hit
        if payload is None:
            pick = hit
        else:
            pick = jnp.max(jnp.where(sel, payload, -1.0), axis=0, keepdims=True)
        out_v = jnp.where(slot == i, m, out_v)
        out_p = jnp.where(slot == i, pick, out_p)
        vals = jnp.where(sel, -jnp.inf, vals)
    return out_v, out_p


def _candidates(sv1, si1, sv2, si2):
    k, w = sv1.shape
    assert k == PEER_TOPK == 2 * SUBLANES
    row8 = lax.broadcasted_iota(jnp.int32, (SUBLANES, w), 0)
    vals = [sv1[0:1, :] + sv2, sv1[1:2, :] + sv2[0:SUBLANES, :]]
    idxs = [si1[0:1, :] * float(N_SUBKEYS) + si2, si1[1:2, :] * float(N_SUBKEYS) + si2[0:SUBLANES, :]]
    for i in range(2, SUBLANES):
        live = row8 < (k // (i + 1))
        vals.append(jnp.where(live, sv1[i:i + 1, :] + sv2[0:SUBLANES, :], -jnp.inf))
        idxs.append(si1[i:i + 1, :] * float(N_SUBKEYS) + si2[0:SUBLANES, :])
    vals.append(sv1[SUBLANES:k, :] + sv2[0:1, :])
    idxs.append(si1[SUBLANES:k, :] * float(N_SUBKEYS) + si2[0:1, :])
    return jnp.concatenate(vals, axis=0), jnp.concatenate(idxs, axis=0)


def _route_kernel(q_ref, keys_ref, idx_ref, gate_ref):
    def head(h, carry):
        col = pl.multiple_of(h * PEER_DQ, PEER_DQ)
        svs, sis = [], []
        for p in range(2):
            qs = q_ref[:, pl.ds(col + p * PEER_DHALF, PEER_DHALF)]
            s = _bdot_nt(keys_ref[h, p], qs)
            sv, si = _topk_rows(s, PEER_TOPK)
            svs.append(sv)
            sis.append(si)
        cand, cidx = _candidates(svs[0], sis[0], svs[1], sis[1])
        cs, ce = _topk_rows(cand, PEER_TOPK, payload=cidx)
        ex = jnp.exp(cs - cs[0:1, :])
        g = ex / jnp.sum(ex, axis=0, keepdims=True)
        row = pl.multiple_of(h * PEER_TOPK, PEER_TOPK)
        idx_ref[pl.ds(row, PEER_TOPK), :] = ce.astype(jnp.int32)
        gate_ref[pl.ds(row, PEER_TOPK), :] = g
        return carry

    lax.fori_loop(0, PEER_HEADS, head, 0)


def _route(q, keys):
    T = q.shape[0]
    tb = _block(T, 256)
    nq = PEER_HEADS * PEER_DQ
    return pl.pallas_call(
        _route_kernel,
        out_shape=(jax.ShapeDtypeStruct((PEER_SEL, T), jnp.int32),
                   jax.ShapeDtypeStruct((PEER_SEL, T), F32)),
        grid=(T // tb,),
        in_specs=[pl.BlockSpec((tb, nq), lambda i: (i, 0)),
                  pl.BlockSpec(keys.shape, lambda i: (0, 0, 0, 0))],
        out_specs=(pl.BlockSpec((PEER_SEL, tb), lambda i: (0, i)),
                   pl.BlockSpec((PEER_SEL, tb), lambda i: (0, i))),
        compiler_params=_params(("arbitrary",), 32),
        name="route",
    )(q, keys.astype(BF16))


GATHER_TOKENS = 64
GATHER_SLOTS = 8
DMA_PRIORITIES = 2


ROW_TILES = D_MODEL // LANES
ACC_CHAINS = 4


def _sum_sublanes_of_each(ps):
    assert len(ps) == SUBLANES == 8
    sub = lax.broadcasted_iota(jnp.int32, ps[0].shape, 0)
    low = sub < 4
    r = []
    for j in range(4):
        a, b = ps[j], ps[j + 4]
        r.append(jnp.where(low, a, b) + pltpu.roll(jnp.where(low, b, a), 4, 0))
    even2 = (sub & 2) == 0
    z = [jnp.where(even2, r[j] + pltpu.roll(r[j], 6, 0), r[j + 2] + pltpu.roll(r[j + 2], 2, 0))
         for j in range(2)]
    even1 = (sub & 1) == 0
    return jnp.where(even1, z[0] + pltpu.roll(z[0], 7, 0), z[1] + pltpu.roll(z[1], 1, 0))


def _gather_kernel(idx_ref, idx_next_ref, xn_ref, h1_ref, gate_ref, uv_hbm, out_ref, uvbuf, sem,
                   *, tbg, steps_per_gate_block):
    step = pl.program_id(0)
    n_steps = pl.num_programs(0)
    lane0 = (step % steps_per_gate_block) * tbg
    ahead = GATHER_SLOTS - 1
    n_groups = tbg // GATHER_SLOTS

    def issue(ids_ref, base, slot):
        for k in range(PEER_SEL):
            e = ids_ref[0, 0, base + k]
            pltpu.make_async_copy(uv_hbm.at[e], uvbuf.at[slot, k],
                                  sem.at[slot]).start(priority=k % DMA_PRIORITIES)

    def wait(slot):
        pltpu.make_async_copy(uvbuf.at[slot], uvbuf.at[slot], sem.at[slot]).wait()

    lanes = lax.broadcasted_iota(jnp.int32, gate_ref.shape, 1)

    def compute(tok, slot):
        x8 = xn_ref[tok]
        gcol = jnp.sum(jnp.where(lanes == lane0 + tok, gate_ref[...], 0.0), axis=1, keepdims=True)
        accs = [None] * ACC_CHAINS
        for grp in range(PEER_SEL // SUBLANES):
            k0 = grp * SUBLANES
            prods = [uvbuf[slot, k0 + j, 0:ROW_TILES, :] * x8 for j in range(SUBLANES)]
            act = jnp.sum(_sum_sublanes_of_each(prods), axis=1, keepdims=True)
            w = jnp.broadcast_to(gcol[k0:k0 + SUBLANES, :] * _gelu_tanh(act), (SUBLANES, LANES))
            for j in range(SUBLANES):
                term = (jnp.broadcast_to(w[j:j + 1, :], (ROW_TILES, LANES))
                        * uvbuf[slot, k0 + j, ROW_TILES:2 * ROW_TILES, :])
                c = (k0 + j) % ACC_CHAINS
                accs[c] = term if accs[c] is None else accs[c] + term
        out_ref[tok] = h1_ref[tok] + functools.reduce(lambda a, b: a + b, accs)

    @pl.when(step == 0)
    def _():
        for s in range(ahead):
            issue(idx_ref, s * PEER_SEL, s)

    def group(g, last):
        for s in range(GATHER_SLOTS):
            tok = g * GATHER_SLOTS + s
            slot_ahead = (s + ahead) % GATHER_SLOTS
            if not last or s == 0:
                issue(idx_ref, (tok + ahead) * PEER_SEL, slot_ahead)
            else:
                @pl.when(step + 1 < n_steps)
                def _():
                    issue(idx_next_ref, (s - 1) * PEER_SEL, slot_ahead)
            wait(s)
            compute(tok, s)

    def body(g, carry):
        group(g, False)
        return carry

    lax.fori_loop(0, n_groups - 1, body, 0)
    group(n_groups - 1, True)


def _gather(eidx, gates, xn, h1, peer_u, peer_v):
    T = xn.shape[0]
    tbg = _block(T, GATHER_TOKENS)
    assert tbg % GATHER_SLOTS == 0
    n_steps = T // tbg
    ahead = GATHER_SLOTS - 1
    gate_tb = _block(T, LANES)
    per = gate_tb // tbg
    n_exp = peer_u.shape[0]
    uv = jnp.concatenate([peer_u.reshape(n_exp, ROW_TILES, LANES),
                          peer_v.reshape(n_exp, ROW_TILES, LANES)], axis=1)
    idx = eidx.T.reshape(n_steps, 1, tbg * PEER_SEL)
    idx_head = idx[:, :, :ahead * PEER_SEL]
    kern = functools.partial(_gather_kernel, tbg=tbg, steps_per_gate_block=per)
    tok_spec = pl.BlockSpec((tbg, ROW_TILES, LANES), lambda i: (i, 0, 0))
    out = pl.pallas_call(
        kern,
        out_shape=jax.ShapeDtypeStruct((T, ROW_TILES, LANES), F32),
        grid=(n_steps,),
        in_specs=[pl.BlockSpec((1, 1, tbg * PEER_SEL), lambda i: (i, 0, 0),
                               memory_space=pltpu.SMEM),
                  pl.BlockSpec((1, 1, ahead * PEER_SEL),
                               lambda i: (jnp.minimum(i + 1, n_steps - 1), 0, 0),
                               memory_space=pltpu.SMEM),
                  tok_spec, tok_spec,
                  pl.BlockSpec((PEER_SEL, gate_tb), lambda i: (0, i // per)),
                  pl.BlockSpec(memory_space=pl.ANY)],
        out_specs=tok_spec,
        scratch_shapes=[pltpu.VMEM((GATHER_SLOTS, PEER_SEL, 2 * ROW_TILES, LANES), F32),
                        pltpu.SemaphoreType.DMA((GATHER_SLOTS,))],
        compiler_params=_params(("arbitrary",), 32),
        name="gather",
    )(idx, idx_head, xn.reshape(T, ROW_TILES, LANES), h1.reshape(T, ROW_TILES, LANES), gates, uv)
    return out.reshape(T, D_MODEL)


def _ple_kernel(h_ref, p_ref, wp_ref, gp_ref, wg_ref, gf_ref, y_ref):
    h2 = h_ref[...]
    gate = _sigmoid(_bdot(_rms(h2, gp_ref[...]), wg_ref[...]))
    h3 = h2 + _bdot(p_ref[...], wp_ref[...]) * gate
    y_ref[...] = _rms(h3, gf_ref[...])


def _ple(h2, p2d, w_ple, g_ple, w_gate, g_final):
    T = h2.shape[0]
    tb = _block(T, 256)
    tok = lambda w: pl.BlockSpec((tb, w), lambda i: (i, 0))
    full = lambda shape: pl.BlockSpec(shape, lambda i: (0, 0))
    return pl.pallas_call(
        _ple_kernel,
        out_shape=jax.ShapeDtypeStruct((T, D_MODEL), F32),
        grid=(T // tb,),
        in_specs=[tok(D_MODEL), tok(PLE_DIM), full((PLE_DIM, D_MODEL)), full((1, D_MODEL)),
                  full((D_MODEL, D_MODEL)), full((1, D_MODEL))],
        out_specs=tok(D_MODEL),
        compiler_params=_params(("arbitrary",), 32),
        name="ple",
    )(h2, p2d, w_ple.astype(BF16), g_ple.reshape(1, D_MODEL), w_gate.astype(BF16),
      g_final.reshape(1, D_MODEL))


def _run_stream(h, p, s0, r0, c0, lw, g_final, lower_bounds, layer):
    B, L, _ = h.shape
    T = B * L
    h2d = h.reshape(T, D_MODEL)
    z = _in_proj(h2d, lw["g_mix"], lw["w_in"])
    z3 = z.reshape(B, L, IN_COLS)
    oa, s_new = _hgrn(z3, lower_bounds, lw["g_hgrn_norm"], s0, layer)
    ob, r_new, c_new = _rglru(z3, lw["conv_w"], lw["conv_b"], lw["w_rg_a"], lw["b_rg_a"],
                              lw["w_rg_x"], lw["b_rg_x"], lw["lambda_rg"], c0, r0)
    h1, xn, q = _out_proj(h2d, oa.reshape(T, W_A), ob.reshape(T, W_B), lw["w_out"],
                          lw["g_ffn"], lw["w_peer_q"])
    eidx, gates = _route(q, lw["peer_keys"])
    h2 = _gather(eidx, gates, xn, h1, lw["peer_u"], lw["peer_v"])
    y = _ple(h2, p.reshape(T, PLE_DIM), lw["w_ple"], lw["g_ple"], lw["w_ple_gate"], g_final)
    return y.reshape(B, L, D_MODEL), s_new, r_new.reshape(B, W_B), c_new


def kernel(x_prompt, x_sample, state_hgrn, state_rglru, state_conv, p_prompt, p_sample, g_mix, w_in, lower_bounds, g_hgrn_norm, conv_w, conv_b, w_rg_a, b_rg_a, w_rg_x, b_rg_x, lambda_rg, w_out, g_ffn, w_peer_q, peer_keys, peer_u, peer_v, g_ple, w_ple_gate, w_ple, g_final):
    depth = w_in.shape[0]
    assert depth == 1, "the final norm is fused into the last layer's kernel; depth 1 only"
    layer = 0
    lw = dict(g_mix=g_mix[layer], w_in=w_in[layer], g_hgrn_norm=g_hgrn_norm[layer],
              conv_w=conv_w[layer], conv_b=conv_b[layer], w_rg_a=w_rg_a[layer], b_rg_a=b_rg_a[layer],
              w_rg_x=w_rg_x[layer], b_rg_x=b_rg_x[layer], lambda_rg=lambda_rg[layer],
              w_out=w_out[layer], g_ffn=g_ffn[layer], w_peer_q=w_peer_q[layer],
              peer_keys=peer_keys[layer], peer_u=peer_u[layer], peer_v=peer_v[layer],
              g_ple=g_ple[layer], w_ple_gate=w_ple_gate[layer], w_ple=w_ple[layer])
    bp = x_prompt.shape[0]
    y_p, s_p, r_p, c_p = _run_stream(
        x_prompt, p_prompt[layer], jnp.zeros((bp, H_A, DK, DV), F32), jnp.zeros((bp, W_B), F32),
        jnp.zeros((bp, CONV_W - 1, W_B), F32), lw, g_final, lower_bounds, layer)
    y_s, s_s, r_s, c_s = _run_stream(
        x_sample, p_sample[layer], state_hgrn[layer], state_rglru[layer], state_conv[layer],
        lw, g_final, lower_bounds, layer)
    return (y_p, y_s, s_p[None], r_p[None], c_p[None], s_s[None], r_s[None], c_s[None])
```

```python
import functools
import math

import jax
import jax.numpy as jnp
from jax import lax
from jax.experimental import pallas as pl
from jax.experimental.pallas import tpu as pltpu

D_MODEL = 1024
W_A = 512
DK = 128
DV = 128
H_A = W_A // DK
W_B = 512
CONV_W = 4
RG_C = 8.0
IN_COLS = 4 * W_A + 2 * W_B
CHUNK = 64
PEER_HEADS = 8
N_SUBKEYS = 128
PEER_TOPK = 16
PEER_DQ = 256
PEER_DHALF = PEER_DQ // 2
PEER_SEL = PEER_HEADS * PEER_TOPK
PLE_DIM = 256
EPS = 1e-6

LANES = 128
SUBLANES = 8
CONV_PAD = SUBLANES
MIB = 1024 * 1024

F32 = jnp.float32
BF16 = jnp.bfloat16


def _block(n, target):
    b = min(n, target)
    while n % b:
        b -= 1
    return b


def _params(semantics, vmem_mib):
    return pltpu.CompilerParams(dimension_semantics=semantics, vmem_limit_bytes=vmem_mib * MIB)


def _rms(x, g):
    ms = jnp.mean(x * x, axis=-1, keepdims=True)
    return x * lax.rsqrt(ms + EPS) * g


def _sigmoid(x):
    return 1.0 / (1.0 + jnp.exp(-x))


def _gelu_tanh(x):
    c = math.sqrt(2.0 / math.pi)
    return x * (0.5 * (1.0 + jnp.tanh(c * (x + 0.044715 * (x * x * x)))))


def _bdot(a, b):
    return jnp.dot(a.astype(BF16), b.astype(BF16), preferred_element_type=F32)


def _bdot_nt(a, b):
    return lax.dot_general(a.astype(BF16), b.astype(BF16), (((1,), (1,)), ((), ())),
                           preferred_element_type=F32)


def _bdot_tn(a, b):
    return lax.dot_general(a.astype(BF16), b.astype(BF16), (((0,), (0,)), ((), ())),
                           preferred_element_type=F32)


def _cumsum_rows(x):
    n = x.shape[0]
    rows = lax.broadcasted_iota(jnp.int32, x.shape, 0)
    s = 1
    while s < n:
        x = x + jnp.where(rows >= s, pltpu.roll(x, s, 0), 0.0)
        s *= 2
    return x


def _linear_scan_rows(a, b):
    n = a.shape[0]
    rows = lax.broadcasted_iota(jnp.int32, a.shape, 0)
    s = 1
    while s < n:
        keep = rows >= s
        b = jnp.where(keep, a * pltpu.roll(b, s, 0) + b, b)
        a = jnp.where(keep, a * pltpu.roll(a, s, 0), a)
        s *= 2
    return a, b


def _in_proj_kernel(h_ref, g_ref, w_ref, z_ref):
    u = _rms(h_ref[...], g_ref[...])
    z_ref[...] = _bdot(u, w_ref[...])


def _in_proj(h2d, g_mix, w_in):
    T = h2d.shape[0]
    tb = _block(T, 256)
    return pl.pallas_call(
        _in_proj_kernel,
        out_shape=jax.ShapeDtypeStruct((T, IN_COLS), F32),
        grid=(T // tb,),
        in_specs=[pl.BlockSpec((tb, D_MODEL), lambda i: (i, 0)),
                  pl.BlockSpec((1, D_MODEL), lambda i: (0, 0)),
                  pl.BlockSpec((D_MODEL, IN_COLS), lambda i: (0, 0))],
        out_specs=pl.BlockSpec((tb, IN_COLS), lambda i: (i, 0)),
        compiler_params=_params(("arbitrary",), 40),
        name="in_proj",
    )(h2d, g_mix.reshape(1, D_MODEL), w_in.astype(BF16))


def _hgrn_kernel(zq_ref, zf_ref, zi_ref, zg_ref, lbw_ref, gn_ref, s0_ref,
                 o_ref, s_out_ref, st_scr, *, chunk, n_chunks, layer):
    t = pl.program_id(1)

    @pl.when(t == 0)
    def _():
        for h in range(H_A):
            st_scr[h] = s0_ref[0, h].T

    lbw = lbw_ref[...]
    ex = jnp.exp(lbw - jnp.max(lbw, axis=0, keepdims=True))
    sm = ex / jnp.sum(ex, axis=0, keepdims=True)
    lb = jnp.sum(sm[0:layer + 1, :], axis=0, keepdims=True)
    gn = gn_ref[...]

    tril = (lax.broadcasted_iota(jnp.int32, (chunk, chunk), 0)
            >= lax.broadcasted_iota(jnp.int32, (chunk, chunk), 1))

    for c in range(n_chunks):
        rows = pl.ds(c * chunk, chunk)
        zq = zq_ref[0, rows, :]
        zf = zf_ref[0, rows, :]
        v = zi_ref[0, rows, :]
        zg = zg_ref[0, rows, :]
        f = lb + (1.0 - lb) * _sigmoid(zf)
        k = 1.0 - f
        q = zq * _sigmoid(zq)
        cum = _cumsum_rows(jnp.log(f))
        qd = q * jnp.exp(cum)
        kd = k * jnp.exp(-cum)
        last = cum[chunk - 1:chunk, :]
        kdec = k * jnp.exp(last - cum)
        dec = jnp.exp(last)
        for h in range(H_A):
            sl = slice(h * DK, (h + 1) * DK)
            att = jnp.where(tril, _bdot_nt(qd[:, sl], kd[:, sl]), 0.0)
            st = st_scr[h]
            o = _bdot(att, v[:, sl]) + _bdot_nt(qd[:, sl], st)
            st_scr[h] = st * dec[:, sl] + _bdot_tn(v[:, sl], kdec[:, sl])
            on = _rms(o, gn[:, sl])
            zgh = zg[:, sl]
            o_ref[0, rows, sl] = on * (zgh * _sigmoid(zgh))

    @pl.when(t == pl.num_programs(1) - 1)
    def _():
        for h in range(H_A):
            s_out_ref[0, h] = st_scr[h].T


def _hgrn(z3, lower_bounds, g_norm, s0, layer):
    B, L, _ = z3.shape
    chunk = min(CHUNK, L)
    n_chunks = _block(L // chunk, 4)
    tb = chunk * n_chunks
    zspec = lambda j: pl.BlockSpec((1, tb, W_A), lambda b, t, j=j: (b, t, j))
    kern = functools.partial(_hgrn_kernel, chunk=chunk, n_chunks=n_chunks, layer=layer)
    return pl.pallas_call(
        kern,
        out_shape=(jax.ShapeDtypeStruct((B, L, W_A), F32),
                   jax.ShapeDtypeStruct((B, H_A, DK, DV), F32)),
        grid=(B, L // tb),
        in_specs=[zspec(0), zspec(1), zspec(2), zspec(3),
                  pl.BlockSpec(lower_bounds.shape, lambda b, t: (0, 0)),
                  pl.BlockSpec((1, W_A), lambda b, t: (0, 0)),
                  pl.BlockSpec((1, H_A, DK, DV), lambda b, t: (b, 0, 0, 0))],
        out_specs=(pl.BlockSpec((1, tb, W_A), lambda b, t: (b, t, 0)),
                   pl.BlockSpec((1, H_A, DK, DV), lambda b, t: (b, 0, 0, 0))),
        scratch_shapes=[pltpu.VMEM((H_A, DV, DK), F32)],
        compiler_params=_params(("arbitrary", "arbitrary"), 32),
        name="hgrn",
    )(z3, z3, z3, z3, lower_bounds, g_norm.reshape(1, W_A), s0)


def _rglru_kernel(zx_ref, zy_ref, cw_ref, cb_ref, wa_ref, ba_ref, wx_ref, bx_ref, lam_ref,
                  buf0_ref, h0_ref, o_ref, hlast_ref, bufnew_ref, xp_scr, hc_scr, *, tb):
    t = pl.program_id(1)
    hist = CONV_W - 1

    @pl.when(t == 0)
    def _():
        xp_scr[CONV_PAD - hist:CONV_PAD, :] = buf0_ref[0]
        hc_scr[...] = h0_ref[0]

    x = zx_ref[0]
    xp_scr[CONV_PAD:CONV_PAD + tb, :] = x
    cw = cw_ref[...]
    acc = xp_scr[CONV_PAD - hist:CONV_PAD - hist + tb, :] * cw[0:1, :]
    for j in range(1, CONV_W):
        acc = acc + xp_scr[CONV_PAD - hist + j:CONV_PAD - hist + j + tb, :] * cw[j:j + 1, :]
    xc = cb_ref[...] + acc
    tail = xp_scr[CONV_PAD + tb - hist:CONV_PAD + tb, :]
    xp_scr[CONV_PAD - hist:CONV_PAD, :] = tail

    r = _sigmoid(_bdot(xc, wa_ref[...]) + ba_ref[...])
    gi = _sigmoid(_bdot(xc, wx_ref[...]) + bx_ref[...])
    nl = -lam_ref[...]
    softplus = jnp.maximum(nl, 0.0) + jnp.log1p(jnp.exp(-jnp.abs(nl)))
    log_a = -RG_C * r * softplus
    a = jnp.exp(log_a)
    one_minus_a2 = -jnp.tanh(log_a) * (a * a + 1.0)
    b = jnp.sqrt(one_minus_a2) * (gi * xc)
    aa, bb = _linear_scan_rows(a, b)
    hseq = bb + aa * hc_scr[...]
    hc_scr[...] = hseq[tb - 1:tb, :]
    o_ref[0] = hseq * _gelu_tanh(zy_ref[0])

    @pl.when(t == pl.num_programs(1) - 1)
    def _():
        hlast_ref[0] = hseq[tb - 1:tb, :]
        bufnew_ref[0] = tail


def _block_diag(w):
    n, c, d = w.shape
    eye = jnp.eye(n, dtype=w.dtype)
    return (eye[:, None, :, None] * w[:, :, None, :]).reshape(n * c, n * d)


def _rglru(z3, conv_w, conv_b, w_a, b_a, w_x, b_x, lam, buf0, h0):
    B, L, _ = z3.shape
    tb = _block(L, 256)
    row = lambda v: v.reshape(1, W_B)
    full = lambda shape: pl.BlockSpec(shape, lambda b, t: (0,) * len(shape))
    kern = functools.partial(_rglru_kernel, tb=tb)
    return pl.pallas_call(
        kern,
        out_shape=(jax.ShapeDtypeStruct((B, L, W_B), F32),
                   jax.ShapeDtypeStruct((B, 1, W_B), F32),
                   jax.ShapeDtypeStruct((B, CONV_W - 1, W_B), F32)),
        grid=(B, L // tb),
        in_specs=[pl.BlockSpec((1, tb, W_B), lambda b, t: (b, t, 4)),
                  pl.BlockSpec((1, tb, W_B), lambda b, t: (b, t, 5)),
                  full((CONV_W, W_B)), full((1, W_B)),
                  full((W_B, W_B)), full((1, W_B)),
                  full((W_B, W_B)), full((1, W_B)), full((1, W_B)),
                  pl.BlockSpec((1, CONV_W - 1, W_B), lambda b, t: (b, 0, 0)),
                  pl.BlockSpec((1, 1, W_B), lambda b, t: (b, 0, 0))],
        out_specs=(pl.BlockSpec((1, tb, W_B), lambda b, t: (b, t, 0)),
                   pl.BlockSpec((1, 1, W_B), lambda b, t: (b, 0, 0)),
                   pl.BlockSpec((1, CONV_W - 1, W_B), lambda b, t: (b, 0, 0))),
        scratch_shapes=[pltpu.VMEM((CONV_PAD + tb, W_B), F32), pltpu.VMEM((1, W_B), F32)],
        compiler_params=_params(("arbitrary", "arbitrary"), 32),
        name="rglru",
    )(z3, z3, conv_w, row(conv_b), _block_diag(w_a).astype(BF16), row(b_a),
      _block_diag(w_x).astype(BF16), row(b_x), row(lam), buf0, h0.reshape(B, 1, W_B))


def _out_proj_kernel(h_ref, oa_ref, ob_ref, wt_ref, wb_ref, g_ref, wq_ref, h1_ref, xn_ref, q_ref):
    h1 = h_ref[...] + (_bdot(oa_ref[...], wt_ref[...]) + _bdot(ob_ref[...], wb_ref[...]))
    h1_ref[...] = h1
    xn = _rms(h1, g_ref[...])
    xn_ref[...] = xn
    q_ref[...] = _bdot(xn, wq_ref[...])


def _out_proj(h2d, oa, ob, w_out, g_ffn, w_q):
    T = h2d.shape[0]
    tb = _block(T, 256)
    nq = PEER_HEADS * PEER_DQ
    w_out = w_out.astype(BF16)
    tok = lambda w: pl.BlockSpec((tb, w), lambda i: (i, 0))
    full = lambda shape: pl.BlockSpec(shape, lambda i: (0, 0))
    return pl.pallas_call(
        _out_proj_kernel,
        out_shape=(jax.ShapeDtypeStruct((T, D_MODEL), F32),
                   jax.ShapeDtypeStruct((T, D_MODEL), F32),
                   jax.ShapeDtypeStruct((T, nq), F32)),
        grid=(T // tb,),
        in_specs=[tok(D_MODEL), tok(W_A), tok(W_B),
                  full((W_A, D_MODEL)), full((W_B, D_MODEL)), full((1, D_MODEL)),
                  full((D_MODEL, nq))],
        out_specs=(tok(D_MODEL), tok(D_MODEL), tok(nq)),
        compiler_params=_params(("arbitrary",), 48),
        name="out_proj",
    )(h2d, oa, ob, w_out[:W_A], w_out[W_A:], g_ffn.reshape(1, D_MODEL), w_q.astype(BF16))


def _topk_rows(vals, k, payload=None):
    n, w = vals.shape
    rows = lax.broadcasted_iota(jnp.int32, (n, w), 0).astype(F32)
    slot = lax.broadcasted_iota(jnp.int32, (k, w), 0)
    out_v = jnp.zeros((k, w), F32)
    out_p = jnp.zeros((k, w), F32)
    for i in range(k):
        m = jnp.max(vals, axis=0, keepdims=True)
        hit = jnp.min(jnp.where(vals == m, rows, float(n)), axis=0, keepdims=True)
        sel = rows == hit
        if payload is None:
            pick = hit
        else:
            pick = jnp.max(jnp.where(sel, payload, -1.0), axis=0, keepdims=True)
        out_v = jnp.where(slot == i, m, out_v)
        out_p = jnp.where(slot == i, pick, out_p)
        vals = jnp.where(sel, -jnp.inf, vals)
    return out_v, out_p


def _candidates(sv1, si1, sv2, si2):
    k, w = sv1.shape
    assert k == PEER_TOPK == 2 * SUBLANES
    row8 = lax.broadcasted_iota(jnp.int32, (SUBLANES, w), 0)
    vals = [sv1[0:1, :] + sv2, sv1[1:2, :] + sv2[0:SUBLANES, :]]
    idxs = [si1[0:1, :] * float(N_SUBKEYS) + si2, si1[1:2, :] * float(N_SUBKEYS) + si2[0:SUBLANES, :]]
    for i in range(2, SUBLANES):
        live = row8 < (k // (i + 1))
        vals.append(jnp.where(live, sv1[i:i + 1, :] + sv2[0:SUBLANES, :], -jnp.inf))
        idxs.append(si1[i:i + 1, :] * float(N_SUBKEYS) + si2[0:SUBLANES, :])
    vals.append(sv1[SUBLANES:k, :] + sv2[0:1, :])
    idxs.append(si1[SUBLANES:k, :] * float(N_SUBKEYS) + si2[0:1, :])
    return jnp.concatenate(vals, axis=0), jnp.concatenate(idxs, axis=0)


def _route_kernel(q_ref, keys_ref, idx_ref, gate_ref):
    def head(h, carry):
        col = pl.multiple_of(h * PEER_DQ, PEER_DQ)
        svs, sis = [], []
        for p in range(2):
            qs = q_ref[:, pl.ds(col + p * PEER_DHALF, PEER_DHALF)]
            s = _bdot_nt(keys_ref[h, p], qs)
            sv, si = _topk_rows(s, PEER_TOPK)
            svs.append(sv)
            sis.append(si)
        cand, cidx = _candidates(svs[0], sis[0], svs[1], sis[1])
        cs, ce = _topk_rows(cand, PEER_TOPK, payload=cidx)
        ex = jnp.exp(cs - cs[0:1, :])
        g = ex / jnp.sum(ex, axis=0, keepdims=True)
        row = pl.multiple_of(h * PEER_TOPK, PEER_TOPK)
        idx_ref[pl.ds(row, PEER_TOPK), :] = ce.astype(jnp.int32)
        gate_ref[pl.ds(row, PEER_TOPK), :] = g
        return carry

    lax.fori_loop(0, PEER_HEADS, head, 0)


def _route(q, keys):
    T = q.shape[0]
    tb = _block(T, 4 * LANES)
    nq = PEER_HEADS * PEER_DQ
    return pl.pallas_call(
        _route_kernel,
        out_shape=(jax.ShapeDtypeStruct((PEER_SEL, T), jnp.int32),
                   jax.ShapeDtypeStruct((PEER_SEL, T), F32)),
        grid=(T // tb,),
        in_specs=[pl.BlockSpec((tb, nq), lambda i: (i, 0)),
                  pl.BlockSpec(keys.shape, lambda i: (0, 0, 0, 0))],
        out_specs=(pl.BlockSpec((PEER_SEL, tb), lambda i: (0, i)),
                   pl.BlockSpec((PEER_SEL, tb), lambda i: (0, i))),
        compiler_params=_params(("arbitrary",), 32),
        name="route",
    )(q, keys.astype(BF16))


GATHER_TOKENS = 64
GATHER_SLOTS = 8
DMA_PRIORITIES = 2


ROW_TILES = D_MODEL // LANES
ACC_CHAINS = 4
ISSUE_LAG = 4


def _sum_sublanes_of_each(ps):
    assert len(ps) == SUBLANES == 8
    sub = lax.broadcasted_iota(jnp.int32, ps[0].shape, 0)
    low = sub < 4
    r = []
    for j in range(4):
        a, b = ps[j], ps[j + 4]
        r.append(jnp.where(low, a, b) + pltpu.roll(jnp.where(low, b, a), 4, 0))
    even2 = (sub & 2) == 0
    z = [jnp.where(even2, r[j] + pltpu.roll(r[j], 6, 0), r[j + 2] + pltpu.roll(r[j + 2], 2, 0))
         for j in range(2)]
    even1 = (sub & 1) == 0
    return jnp.where(even1, z[0] + pltpu.roll(z[0], 7, 0), z[1] + pltpu.roll(z[1], 1, 0))


def _gather_kernel(idx_ref, idx_next_ref, xn_ref, h1_ref, gate_ref, uv_hbm, out_ref, uvbuf, sem,
                   *, tbg, steps_per_gate_block):
    step = pl.program_id(0)
    n_steps = pl.num_programs(0)
    lane0 = (step % steps_per_gate_block) * tbg
    ahead = GATHER_SLOTS - 1

    def issue(ids_ref, base, slot, k_lo=0, k_hi=PEER_SEL):
        ids = ids_ref.at[0, 0, pl.ds(base, PEER_SEL)]
        for k in range(k_lo, k_hi):
            pltpu.make_async_copy(uv_hbm.at[ids[k]], uvbuf.at[slot, k],
                                  sem.at[slot]).start(priority=k % DMA_PRIORITIES)

    def wait(slot):
        pltpu.make_async_copy(uvbuf.at[slot], uvbuf.at[slot], sem.at[slot]).wait()

    lanes = lax.broadcasted_iota(jnp.int32, (SUBLANES, gate_ref.shape[1]), 1)

    def token(tok, ids_ref, ids_base):
        slot = lax.rem(tok, GATHER_SLOTS)
        slot_ahead = lax.rem(tok + ahead, GATHER_SLOTS)
        wait(slot)
        x8 = xn_ref[tok]
        lane = lane0 + tok
        accs = [None] * ACC_CHAINS
        n_grp = PEER_SEL // SUBLANES
        for grp in range(n_grp):
            k0 = grp * SUBLANES
            prods = [uvbuf[slot, k0 + j, 0:ROW_TILES, :] * x8 for j in range(SUBLANES)]
            g8 = jnp.sum(jnp.where(lanes == lane, gate_ref[k0:k0 + SUBLANES, :], 0.0),
                         axis=1, keepdims=True)
            chunks = [grp - ISSUE_LAG] if grp < n_grp - 1 else range(grp - ISSUE_LAG, n_grp)
            for c in chunks:
                if c >= 0:
                    issue(ids_ref, ids_base, slot_ahead, c * SUBLANES, (c + 1) * SUBLANES)
            act = jnp.sum(_sum_sublanes_of_each(prods), axis=1, keepdims=True)
            w = jnp.broadcast_to(g8 * _gelu_tanh(act), (SUBLANES, LANES))
            for j in range(SUBLANES):
                term = (jnp.broadcast_to(w[j:j + 1, :], (ROW_TILES, LANES))
                        * uvbuf[slot, k0 + j, ROW_TILES:2 * ROW_TILES, :])
                c = (k0 + j) % ACC_CHAINS
                accs[c] = term if accs[c] is None else accs[c] + term
        out_ref[tok] = h1_ref[tok] + functools.reduce(lambda a, b: a + b, accs)

    @pl.when(step == 0)
    def _():
        for s in range(ahead):
            issue(idx_ref, s * PEER_SEL, s)

    def body_same_step(tok, carry):
        token(tok, idx_ref, (tok + ahead) * PEER_SEL)
        return carry

    def body_next_step(tok, carry):
        token(tok, idx_next_ref, (tok + ahead - tbg) * PEER_SEL)
        return carry

    lax.fori_loop(0, tbg - ahead, body_same_step, 0)
    lax.fori_loop(tbg - ahead, tbg, body_next_step, 0)

    @pl.when(step == n_steps - 1)
    def _():
        for s in range(ahead):
            wait(s)


def _gather(eidx, gates, xn, h1, peer_u, peer_v):
    T = xn.shape[0]
    tbg = _block(T, GATHER_TOKENS)
    assert tbg % GATHER_SLOTS == 0
    n_steps = T // tbg
    ahead = GATHER_SLOTS - 1
    gate_tb = _block(T, LANES)
    per = gate_tb // tbg
    n_exp = peer_u.shape[0]
    uv = jnp.concatenate([peer_u.reshape(n_exp, ROW_TILES, LANES),
                          peer_v.reshape(n_exp, ROW_TILES, LANES)], axis=1)
    idx = eidx.T.reshape(n_steps, 1, tbg * PEER_SEL)
    idx_head = idx[:, :, :ahead * PEER_SEL]
    kern = functools.partial(_gather_kernel, tbg=tbg, steps_per_gate_block=per)
    tok_spec = pl.BlockSpec((tbg, ROW_TILES, LANES), lambda i: (i, 0, 0))
    out = pl.pallas_call(
        kern,
        out_shape=jax.ShapeDtypeStruct((T, ROW_TILES, LANES), F32),
        grid=(n_steps,),
        in_specs=[pl.BlockSpec((1, 1, tbg * PEER_SEL), lambda i: (i, 0, 0),
                               memory_space=pltpu.SMEM),
                  pl.BlockSpec((1, 1, ahead * PEER_SEL),
                               lambda i: (jnp.minimum(i + 1, n_steps - 1), 0, 0),
                               memory_space=pltpu.SMEM),
                  tok_spec, tok_spec,
                  pl.BlockSpec((PEER_SEL, gate_tb), lambda i: (0, i // per)),
                  pl.BlockSpec(memory_space=pl.ANY)],
        out_specs=tok_spec,
        scratch_shapes=[pltpu.VMEM((GATHER_SLOTS, PEER_SEL, 2 * ROW_TILES, LANES), F32),
                        pltpu.SemaphoreType.DMA((GATHER_SLOTS,))],
        compiler_params=_params(("arbitrary",), 32),
        name="gather",
    )(idx, idx_head, xn.reshape(T, ROW_TILES, LANES), h1.reshape(T, ROW_TILES, LANES), gates, uv)
    return out.reshape(T, D_MODEL)


def _ple_kernel(h_ref, p_ref, wp_ref, gp_ref, wg_ref, gf_ref, y_ref):
    h2 = h_ref[...]
    gate = _sigmoid(_bdot(_rms(h2, gp_ref[...]), wg_ref[...]))
    h3 = h2 + _bdot(p_ref[...], wp_ref[...]) * gate
    y_ref[...] = _rms(h3, gf_ref[...])


def _ple(h2, p2d, w_ple, g_ple, w_gate, g_final):
    T = h2.shape[0]
    tb = _block(T, 256)
    tok = lambda w: pl.BlockSpec((tb, w), lambda i: (i, 0))
    full = lambda shape: pl.BlockSpec(shape, lambda i: (0, 0))
    return pl.pallas_call(
        _ple_kernel,
        out_shape=jax.ShapeDtypeStruct((T, D_MODEL), F32),
        grid=(T // tb,),
        in_specs=[tok(D_MODEL), tok(PLE_DIM), full((PLE_DIM, D_MODEL)), full((1, D_MODEL)),
                  full((D_MODEL, D_MODEL)), full((1, D_MODEL))],
        out_specs=tok(D_MODEL),
        compiler_params=_params(("arbitrary",), 32),
        name="ple",
    )(h2, p2d, w_ple.astype(BF16), g_ple.reshape(1, D_MODEL), w_gate.astype(BF16),
      g_final.reshape(1, D_MODEL))


def _run_stream(h, p, s0, r0, c0, lw, g_final, lower_bounds, layer):
    B, L, _ = h.shape
    T = B * L
    h2d = h.reshape(T, D_MODEL)
    z = _in_proj(h2d, lw["g_mix"], lw["w_in"])
    z3 = z.reshape(B, L, IN_COLS)
    oa, s_new = _hgrn(z3, lower_bounds, lw["g_hgrn_norm"], s0, layer)
    ob, r_new, c_new = _rglru(z3, lw["conv_w"], lw["conv_b"], lw["w_rg_a"], lw["b_rg_a"],
                              lw["w_rg_x"], lw["b_rg_x"], lw["lambda_rg"], c0, r0)
    h1, xn, q = _out_proj(h2d, oa.reshape(T, W_A), ob.reshape(T, W_B), lw["w_out"],
                          lw["g_ffn"], lw["w_peer_q"])
    eidx, gates = _route(q, lw["peer_keys"])
    h2 = _gather(eidx, gates, xn, h1, lw["peer_u"], lw["peer_v"])
    y = _ple(h2, p.reshape(T, PLE_DIM), lw["w_ple"], lw["g_ple"], lw["w_ple_gate"], g_final)
    return y.reshape(B, L, D_MODEL), s_new, r_new.reshape(B, W_B), c_new


def kernel(x_prompt, x_sample, state_hgrn, state_rglru, state_conv, p_prompt, p_sample, g_mix, w_in, lower_bounds, g_hgrn_norm, conv_w, conv_b, w_rg_a, b_rg_a, w_rg_x, b_rg_x, lambda_rg, w_out, g_ffn, w_peer_q, peer_keys, peer_u, peer_v, g_ple, w_ple_gate, w_ple, g_final):
    depth = w_in.shape[0]
    assert depth == 1, "the final norm is fused into the last layer's kernel; depth 1 only"
    layer = 0
    lw = dict(g_mix=g_mix[layer], w_in=w_in[layer], g_hgrn_norm=g_hgrn_norm[layer],
              conv_w=conv_w[layer], conv_b=conv_b[layer], w_rg_a=w_rg_a[layer], b_rg_a=b_rg_a[layer],
              w_rg_x=w_rg_x[layer], b_rg_x=b_rg_x[layer], lambda_rg=lambda_rg[layer],
              w_out=w_out[layer], g_ffn=g_ffn[layer], w_peer_q=w_peer_q[layer],
              peer_keys=peer_keys[layer], peer_u=peer_u[layer], peer_v=peer_v[layer],
              g_ple=g_ple[layer], w_ple_gate=w_ple_gate[layer], w_ple=w_ple[layer])
    bp = x_prompt.shape[0]
    y_p, s_p, r_p, c_p = _run_stream(
        x_prompt, p_prompt[layer], jnp.zeros((bp, H_A, DK, DV), F32), jnp.zeros((bp, W_B), F32),
        jnp.zeros((bp, CONV_W - 1, W_B), F32), lw, g_final, lower_bounds, layer)
    y_s, s_s, r_s, c_s = _run_stream(
        x_sample, p_sample[layer], state_hgrn[layer], state_rglru[layer], state_conv[layer],
        lw, g_final, lower_bounds, layer)
    return (y_p, y_s, s_p[None], r_p[None], c_p[None], s_s[None], r_s[None], c_s[None])
```

```python
import functools
import math

import jax
import jax.numpy as jnp
from jax import lax
from jax.experimental import pallas as pl
from jax.experimental.pallas import tpu as pltpu

D_MODEL = 1024
W_A = 512
DK = 128
DV = 128
H_A = W_A // DK
W_B = 512
CONV_W = 4
RG_C = 8.0
IN_COLS = 4 * W_A + 2 * W_B
CHUNK = 64
PEER_HEADS = 8
N_SUBKEYS = 128
PEER_TOPK = 16
PEER_DQ = 256
PEER_DHALF = PEER_DQ // 2
PEER_SEL = PEER_HEADS * PEER_TOPK
PLE_DIM = 256
EPS = 1e-6

LANES = 128
SUBLANES = 8
CONV_PAD = SUBLANES
ROW_TILES = D_MODEL // LANES
MIB = 1024 * 1024

F32 = jnp.float32
BF16 = jnp.bfloat16


def _block(n, target):
    b = min(n, target)
    while n % b:
        b -= 1
    return b


def _params(semantics, vmem_mib):
    return pltpu.CompilerParams(dimension_semantics=semantics, vmem_limit_bytes=vmem_mib * MIB)


def _rms(x, g):
    ms = jnp.mean(x * x, axis=-1, keepdims=True)
    return x * lax.rsqrt(ms + EPS) * g


def _sigmoid(x):
    return 1.0 / (1.0 + jnp.exp(-x))


def _gelu_tanh(x):
    c = math.sqrt(2.0 / math.pi)
    return x * (0.5 * (1.0 + jnp.tanh(c * (x + 0.044715 * (x * x * x)))))


def _bdot(a, b):
    return jnp.dot(a.astype(BF16), b.astype(BF16), preferred_element_type=F32)


def _bdot_nt(a, b):
    return lax.dot_general(a.astype(BF16), b.astype(BF16), (((1,), (1,)), ((), ())),
                           preferred_element_type=F32)


def _bdot_tn(a, b):
    return lax.dot_general(a.astype(BF16), b.astype(BF16), (((0,), (0,)), ((), ())),
                           preferred_element_type=F32)


def _cumsum_rows(x):
    n = x.shape[0]
    rows = lax.broadcasted_iota(jnp.int32, x.shape, 0)
    s = 1
    while s < n:
        x = x + jnp.where(rows >= s, pltpu.roll(x, s, 0), 0.0)
        s *= 2
    return x


def _linear_scan_rows(a, b):
    n = a.shape[0]
    rows = lax.broadcasted_iota(jnp.int32, a.shape, 0)
    s = 1
    while s < n:
        keep = rows >= s
        b = jnp.where(keep, a * pltpu.roll(b, s, 0) + b, b)
        a = jnp.where(keep, a * pltpu.roll(a, s, 0), a)
        s *= 2
    return a, b


def _hgrn_chunk(zq, zf, v, zg, lb, gn, tril, st_scr):
    chunk = zq.shape[0]
    f = lb + (1.0 - lb) * _sigmoid(zf)
    k = 1.0 - f
    q = zq * _sigmoid(zq)
    cum = _cumsum_rows(jnp.log(f))
    qd = q * jnp.exp(cum)
    kd = k * jnp.exp(-cum)
    last = cum[chunk - 1:chunk, :]
    kdec = k * jnp.exp(last - cum)
    dec = jnp.exp(last)
    outs = []
    for h in range(H_A):
        sl = slice(h * DK, (h + 1) * DK)
        att = jnp.where(tril, _bdot_nt(qd[:, sl], kd[:, sl]), 0.0)
        st = st_scr[h]
        o = _bdot(att, v[:, sl]) + _bdot_nt(qd[:, sl], st)
        st_scr[h] = st * dec[:, sl] + _bdot_tn(v[:, sl], kdec[:, sl])
        zgh = zg[:, sl]
        outs.append(_rms(o, gn[:, sl]) * (zgh * _sigmoid(zgh)))
    return outs


def _mixer_kernel(h_ref, gm_ref, win_ref, lbw_ref, gn_ref, cw_ref, cb_ref, wa_ref, ba_ref, wx_ref,
                  bx_ref, lam_ref, wout_ref, gf_ref, wq_ref, s0_ref, buf0_ref, r0_ref,
                  h1_ref, xn_ref, q_ref, s_out_ref, rlast_ref, bufnew_ref,
                  z_scr, o_scr, st_scr, xp_scr, hc_scr, *, tb, chunk, layer):
    t = pl.program_id(1)
    hist = CONV_W - 1

    @pl.when(t == 0)
    def _():
        for h in range(H_A):
            st_scr[h] = s0_ref[0, h].T
        xp_scr[CONV_PAD - hist:CONV_PAD, :] = buf0_ref[0]
        hc_scr[...] = r0_ref[0]

    hin = h_ref[0]
    z_scr[...] = _bdot(_rms(hin, gm_ref[...]), win_ref[...])

    lbw = lbw_ref[...]
    ex = jnp.exp(lbw - jnp.max(lbw, axis=0, keepdims=True))
    sm = ex / jnp.sum(ex, axis=0, keepdims=True)
    lb = jnp.sum(sm[0:layer + 1, :], axis=0, keepdims=True)
    gn = gn_ref[...]
    tril = (lax.broadcasted_iota(jnp.int32, (chunk, chunk), 0)
            >= lax.broadcasted_iota(jnp.int32, (chunk, chunk), 1))
    for c in range(tb // chunk):
        rows = pl.ds(c * chunk, chunk)
        zs = [z_scr[rows, j * W_A:(j + 1) * W_A] for j in range(4)]
        outs = _hgrn_chunk(zs[0], zs[1], zs[2], zs[3], lb, gn, tril, st_scr)
        for h in range(H_A):
            o_scr[rows, h * DK:(h + 1) * DK] = outs[h]

    x = z_scr[:, 4 * W_A:4 * W_A + W_B]
    xp_scr[CONV_PAD:CONV_PAD + tb, :] = x
    cw = cw_ref[...]
    acc = xp_scr[CONV_PAD - hist:CONV_PAD - hist + tb, :] * cw[0:1, :]
    for j in range(1, CONV_W):
        acc = acc + xp_scr[CONV_PAD - hist + j:CONV_PAD - hist + j + tb, :] * cw[j:j + 1, :]
    xc = cb_ref[...] + acc
    tail = xp_scr[CONV_PAD + tb - hist:CONV_PAD + tb, :]
    xp_scr[CONV_PAD - hist:CONV_PAD, :] = tail

    r = _sigmoid(_bdot(xc, wa_ref[...]) + ba_ref[...])
    gi = _sigmoid(_bdot(xc, wx_ref[...]) + bx_ref[...])
    nl = -lam_ref[...]
    softplus = jnp.maximum(nl, 0.0) + jnp.log1p(jnp.exp(-jnp.abs(nl)))
    log_a = -RG_C * r * softplus
    a = jnp.exp(log_a)
    one_minus_a2 = -jnp.tanh(log_a) * (a * a + 1.0)
    b = jnp.sqrt(one_minus_a2) * (gi * xc)
    aa, bb = _linear_scan_rows(a, b)
    hseq = bb + aa * hc_scr[...]
    hc_scr[...] = hseq[tb - 1:tb, :]
    o_scr[:, W_A:W_A + W_B] = hseq * _gelu_tanh(z_scr[:, 4 * W_A + W_B:4 * W_A + 2 * W_B])

    h1 = hin + _bdot(o_scr[...], wout_ref[...])
    xn = _rms(h1, gf_ref[...])
    for s in range(ROW_TILES):
        h1_ref[0, :, s, :] = h1[:, s * LANES:(s + 1) * LANES]
        xn_ref[0, :, s, :] = xn[:, s * LANES:(s + 1) * LANES]
    q_ref[0] = _bdot(xn, wq_ref[...])

    @pl.when(t == pl.num_programs(1) - 1)
    def _():
        for h in range(H_A):
            s_out_ref[0, h] = st_scr[h].T
        rlast_ref[0] = hseq[tb - 1:tb, :]
        bufnew_ref[0] = tail


def _block_diag(w):
    n, c, d = w.shape
    eye = jnp.eye(n, dtype=w.dtype)
    return (eye[:, None, :, None] * w[:, :, None, :]).reshape(n * c, n * d)


def _mixer(h, lw, lower_bounds, s0, buf0, r0, layer):
    B, L, _ = h.shape
    chunk = min(CHUNK, L)
    tb = chunk * _block(L // chunk, 4)
    nq = PEER_HEADS * PEER_DQ
    row = lambda v, w: v.reshape(1, w)
    full = lambda shape: pl.BlockSpec(shape, lambda b, t: (0,) * len(shape))
    per_batch = lambda shape: pl.BlockSpec((1,) + shape, lambda b, t: (b,) + (0,) * len(shape))
    tok3 = pl.BlockSpec((1, tb, ROW_TILES, LANES), lambda b, t: (b, t, 0, 0))
    kern = functools.partial(_mixer_kernel, tb=tb, chunk=chunk, layer=layer)
    h1, xn, q, s_new, r_new, c_new = pl.pallas_call(
        kern,
        out_shape=(jax.ShapeDtypeStruct((B, L, ROW_TILES, LANES), F32),
                   jax.ShapeDtypeStruct((B, L, ROW_TILES, LANES), F32),
                   jax.ShapeDtypeStruct((B, L, nq), F32),
                   jax.ShapeDtypeStruct((B, H_A, DK, DV), F32),
                   jax.ShapeDtypeStruct((B, 1, W_B), F32),
                   jax.ShapeDtypeStruct((B, CONV_W - 1, W_B), F32)),
        grid=(B, L // tb),
        in_specs=[pl.BlockSpec((1, tb, D_MODEL), lambda b, t: (b, t, 0)),
                  full((1, D_MODEL)), full((D_MODEL, IN_COLS)),
                  full(lower_bounds.shape), full((1, W_A)),
                  full((CONV_W, W_B)), full((1, W_B)),
                  full((W_B, W_B)), full((1, W_B)), full((W_B, W_B)), full((1, W_B)), full((1, W_B)),
                  full((W_A + W_B, D_MODEL)), full((1, D_MODEL)), full((D_MODEL, nq)),
                  per_batch((H_A, DK, DV)), per_batch((CONV_W - 1, W_B)), per_batch((1, W_B))],
        out_specs=(tok3, tok3,
                   pl.BlockSpec((1, tb, nq), lambda b, t: (b, t, 0)),
                   per_batch((H_A, DK, DV)), per_batch((1, W_B)), per_batch((CONV_W - 1, W_B))),
        scratch_shapes=[pltpu.VMEM((tb, IN_COLS), F32), pltpu.VMEM((tb, W_A + W_B), F32),
                        pltpu.VMEM((H_A, DV, DK), F32),
                        pltpu.VMEM((CONV_PAD + tb, W_B), F32), pltpu.VMEM((1, W_B), F32)],
        compiler_params=_params(("arbitrary", "arbitrary"), 56),
        name="mixer",
    )(h, row(lw["g_mix"], D_MODEL), lw["w_in"].astype(BF16), lower_bounds,
      row(lw["g_hgrn_norm"], W_A), lw["conv_w"], row(lw["conv_b"], W_B),
      _block_diag(lw["w_rg_a"]).astype(BF16), row(lw["b_rg_a"], W_B),
      _block_diag(lw["w_rg_x"]).astype(BF16), row(lw["b_rg_x"], W_B), row(lw["lambda_rg"], W_B),
      lw["w_out"].astype(BF16), row(lw["g_ffn"], D_MODEL), lw["w_peer_q"].astype(BF16),
      s0, buf0, r0.reshape(B, 1, W_B))
    T = B * L
    return (h1.reshape(T, ROW_TILES, LANES), xn.reshape(T, ROW_TILES, LANES), q.reshape(T, nq),
            s_new, r_new.reshape(B, W_B), c_new)


def _topk_rows(vals, k, payload=None):
    n, w = vals.shape
    rows = lax.broadcasted_iota(jnp.int32, (n, w), 0).astype(F32)
    slot = lax.broadcasted_iota(jnp.int32, (k, w), 0)
    out_v = jnp.zeros((k, w), F32)
    out_p = jnp.zeros((k, w), F32)
    for i in range(k):
        m = jnp.max(vals, axis=0, keepdims=True)
        hit = jnp.min(jnp.where(vals == m, rows, float(n)), axis=0, keepdims=True)
        sel = rows == hit
        if payload is None:
            pick = hit
        else:
            pick = jnp.max(jnp.where(sel, payload, -1.0), axis=0, keepdims=True)
        out_v = jnp.where(slot == i, m, out_v)
        out_p = jnp.where(slot == i, pick, out_p)
        vals = jnp.where(sel, -jnp.inf, vals)
    return out_v, out_p


def _candidates(sv1, si1, sv2, si2):
    k, w = sv1.shape
    assert k == PEER_TOPK == 2 * SUBLANES
    row8 = lax.broadcasted_iota(jnp.int32, (SUBLANES, w), 0)
    vals = [sv1[0:1, :] + sv2, sv1[1:2, :] + sv2[0:SUBLANES, :]]
    idxs = [si1[0:1, :] * float(N_SUBKEYS) + si2, si1[1:2, :] * float(N_SUBKEYS) + si2[0:SUBLANES, :]]
    for i in range(2, SUBLANES):
        live = row8 < (k // (i + 1))
        vals.append(jnp.where(live, sv1[i:i + 1, :] + sv2[0:SUBLANES, :], -jnp.inf))
        idxs.append(si1[i:i + 1, :] * float(N_SUBKEYS) + si2[0:SUBLANES, :])
    vals.append(sv1[SUBLANES:k, :] + sv2[0:1, :])
    idxs.append(si1[SUBLANES:k, :] * float(N_SUBKEYS) + si2[0:1, :])
    return jnp.concatenate(vals, axis=0), jnp.concatenate(idxs, axis=0)


def _route_kernel(q_ref, keys_ref, idx_ref, gate_ref):
    def head(h, carry):
        col = pl.multiple_of(h * PEER_DQ, PEER_DQ)
        svs, sis = [], []
        for p in range(2):
            qs = q_ref[:, pl.ds(col + p * PEER_DHALF, PEER_DHALF)]
            s = _bdot_nt(keys_ref[h, p], qs)
            sv, si = _topk_rows(s, PEER_TOPK)
            svs.append(sv)
            sis.append(si)
        cand, cidx = _candidates(svs[0], sis[0], svs[1], sis[1])
        cs, ce = _topk_rows(cand, PEER_TOPK, payload=cidx)
        ex = jnp.exp(cs - cs[0:1, :])
        g = ex / jnp.sum(ex, axis=0, keepdims=True)
        row = pl.multiple_of(h * PEER_TOPK, PEER_TOPK)
        idx_ref[pl.ds(row, PEER_TOPK), :] = ce.astype(jnp.int32)
        gate_ref[pl.ds(row, PEER_TOPK), :] = g
        return carry

    lax.fori_loop(0, PEER_HEADS, head, 0)


def _route(q, keys):
    T = q.shape[0]
    tb = _block(T, 4 * LANES)
    nq = PEER_HEADS * PEER_DQ
    return pl.pallas_call(
        _route_kernel,
        out_shape=(jax.ShapeDtypeStruct((PEER_SEL, T), jnp.int32),
                   jax.ShapeDtypeStruct((PEER_SEL, T), F32)),
        grid=(T // tb,),
        in_specs=[pl.BlockSpec((tb, nq), lambda i: (i, 0)),
                  pl.BlockSpec(keys.shape, lambda i: (0, 0, 0, 0))],
        out_specs=(pl.BlockSpec((PEER_SEL, tb), lambda i: (0, i)),
                   pl.BlockSpec((PEER_SEL, tb), lambda i: (0, i))),
        compiler_params=_params(("arbitrary",), 32),
        name="route",
    )(q, keys.astype(BF16))


GATHER_TOKENS = 64
GATHER_SLOTS = 8
DMA_PRIORITIES = 2


ACC_CHAINS = 4
ISSUE_LAG = 6


def _sum_sublanes_of_each(ps):
    assert len(ps) == SUBLANES == 8
    sub = lax.broadcasted_iota(jnp.int32, ps[0].shape, 0)
    low = sub < 4
    r = []
    for j in range(4):
        a, b = ps[j], ps[j + 4]
        r.append(jnp.where(low, a, b) + pltpu.roll(jnp.where(low, b, a), 4, 0))
    even2 = (sub & 2) == 0
    z = [jnp.where(even2, r[j] + pltpu.roll(r[j], 6, 0), r[j + 2] + pltpu.roll(r[j + 2], 2, 0))
         for j in range(2)]
    even1 = (sub & 1) == 0
    return jnp.where(even1, z[0] + pltpu.roll(z[0], 7, 0), z[1] + pltpu.roll(z[1], 1, 0))


def _gather_kernel(idx_ref, idx_next_ref, xn_ref, h1_ref, gate_ref, uv_hbm, out_ref, uvbuf, sem,
                   *, tbg, steps_per_gate_block):
    step = pl.program_id(0)
    n_steps = pl.num_programs(0)
    lane0 = (step % steps_per_gate_block) * tbg
    ahead = GATHER_SLOTS - 1

    def issue(ids_ref, base, slot, k_lo=0, k_hi=PEER_SEL):
        ids = ids_ref.at[0, 0, pl.ds(base, PEER_SEL)]
        for k in range(k_lo, k_hi):
            pltpu.make_async_copy(uv_hbm.at[ids[k]], uvbuf.at[slot, k],
                                  sem.at[slot]).start(priority=k % DMA_PRIORITIES)

    def wait(slot):
        pltpu.make_async_copy(uvbuf.at[slot], uvbuf.at[slot], sem.at[slot]).wait()

    lanes = lax.broadcasted_iota(jnp.int32, (SUBLANES, gate_ref.shape[1]), 1)

    def token(tok, ids_ref, ids_base):
        slot = tok & (GATHER_SLOTS - 1)
        slot_ahead = (tok + ahead) & (GATHER_SLOTS - 1)
        wait(slot)
        x8 = xn_ref[tok]
        lane = lane0 + tok
        accs = [None] * ACC_CHAINS
        n_grp = PEER_SEL // SUBLANES
        for grp in range(n_grp):
            k0 = grp * SUBLANES
            prods = [uvbuf[slot, k0 + j, 0:ROW_TILES, :] * x8 for j in range(SUBLANES)]
            g8 = jnp.sum(jnp.where(lanes == lane, gate_ref[k0:k0 + SUBLANES, :], 0.0),
                         axis=1, keepdims=True)
            chunks = [grp - ISSUE_LAG] if grp < n_grp - 1 else range(grp - ISSUE_LAG, n_grp)
            for c in chunks:
                if c >= 0:
                    issue(ids_ref, ids_base, slot_ahead, c * SUBLANES, (c + 1) * SUBLANES)
            act = jnp.sum(_sum_sublanes_of_each(prods), axis=1, keepdims=True)
            w = jnp.broadcast_to(g8 * _gelu_tanh(act), (SUBLANES, LANES))
            for j in range(SUBLANES):
                term = (jnp.broadcast_to(w[j:j + 1, :], (ROW_TILES, LANES))
                        * uvbuf[slot, k0 + j, ROW_TILES:2 * ROW_TILES, :])
                c = (k0 + j) % ACC_CHAINS
                accs[c] = term if accs[c] is None else accs[c] + term
        out_ref[tok] = h1_ref[tok] + functools.reduce(lambda a, b: a + b, accs)

    @pl.when(step == 0)
    def _():
        for s in range(ahead):
            issue(idx_ref, s * PEER_SEL, s)

    def body_same_step(tok, carry):
        token(tok, idx_ref, (tok + ahead) * PEER_SEL)
        return carry

    def body_next_step(tok, carry):
        token(tok, idx_next_ref, (tok + ahead - tbg) * PEER_SEL)
        return carry

    lax.fori_loop(0, tbg - ahead, body_same_step, 0)
    lax.fori_loop(tbg - ahead, tbg, body_next_step, 0)

    @pl.when(step == n_steps - 1)
    def _():
        for s in range(ahead):
            wait(s)


def _gather(eidx, gates, xn, h1, peer_u, peer_v):
    T = xn.shape[0]
    tbg = _block(T, GATHER_TOKENS)
    assert tbg % GATHER_SLOTS == 0 and GATHER_SLOTS & (GATHER_SLOTS - 1) == 0
    n_steps = T // tbg
    ahead = GATHER_SLOTS - 1
    gate_tb = _block(T, LANES)
    per = gate_tb // tbg
    n_exp = peer_u.shape[0]
    uv = jnp.concatenate([peer_u.reshape(n_exp, ROW_TILES, LANES),
                          peer_v.reshape(n_exp, ROW_TILES, LANES)], axis=1)
    idx = eidx.T.reshape(n_steps, 1, tbg * PEER_SEL)
    idx_head = idx[:, :, :ahead * PEER_SEL]
    kern = functools.partial(_gather_kernel, tbg=tbg, steps_per_gate_block=per)
    tok_spec = pl.BlockSpec((tbg, ROW_TILES, LANES), lambda i: (i, 0, 0))
    out = pl.pallas_call(
        kern,
        out_shape=jax.ShapeDtypeStruct((T, ROW_TILES, LANES), F32),
        grid=(n_steps,),
        in_specs=[pl.BlockSpec((1, 1, tbg * PEER_SEL), lambda i: (i, 0, 0),
                               memory_space=pltpu.SMEM),
                  pl.BlockSpec((1, 1, ahead * PEER_SEL),
                               lambda i: (jnp.minimum(i + 1, n_steps - 1), 0, 0),
                               memory_space=pltpu.SMEM),
                  tok_spec, tok_spec,
                  pl.BlockSpec((PEER_SEL, gate_tb), lambda i: (0, i // per)),
                  pl.BlockSpec(memory_space=pl.ANY)],
        out_specs=tok_spec,
        scratch_shapes=[pltpu.VMEM((GATHER_SLOTS, PEER_SEL, 2 * ROW_TILES, LANES), F32),
                        pltpu.SemaphoreType.DMA((GATHER_SLOTS,))],
        compiler_params=_params(("arbitrary",), 32),
        name="gather",
    )(idx, idx_head, xn, h1, gates, uv)
    return out


def _ple_kernel(h_ref, p_ref, wp_ref, gp_ref, wg_ref, gf_ref, y_ref):
    h2 = jnp.concatenate([h_ref[:, s, :] for s in range(ROW_TILES)], axis=1)
    gate = _sigmoid(_bdot(_rms(h2, gp_ref[...]), wg_ref[...]))
    h3 = h2 + _bdot(p_ref[...], wp_ref[...]) * gate
    y_ref[...] = _rms(h3, gf_ref[...])


def _ple(h2, p2d, w_ple, g_ple, w_gate, g_final):
    T = h2.shape[0]
    tb = _block(T, 256)
    tok = lambda w: pl.BlockSpec((tb, w), lambda i: (i, 0))
    full = lambda shape: pl.BlockSpec(shape, lambda i: (0, 0))
    return pl.pallas_call(
        _ple_kernel,
        out_shape=jax.ShapeDtypeStruct((T, D_MODEL), F32),
        grid=(T // tb,),
        in_specs=[pl.BlockSpec((tb, ROW_TILES, LANES), lambda i: (i, 0, 0)),
                  tok(PLE_DIM), full((PLE_DIM, D_MODEL)), full((1, D_MODEL)),
                  full((D_MODEL, D_MODEL)), full((1, D_MODEL))],
        out_specs=tok(D_MODEL),
        compiler_params=_params(("arbitrary",), 32),
        name="ple",
    )(h2, p2d, w_ple.astype(BF16), g_ple.reshape(1, D_MODEL), w_gate.astype(BF16),
      g_final.reshape(1, D_MODEL))


def _run_stream(h, p, s0, r0, c0, lw, g_final, lower_bounds, layer):
    B, L, _ = h.shape
    T = B * L
    h1, xn, q, s_new, r_new, c_new = _mixer(h, lw, lower_bounds, s0, c0, r0, layer)
    eidx, gates = _route(q, lw["peer_keys"])
    h2 = _gather(eidx, gates, xn, h1, lw["peer_u"], lw["peer_v"])
    y = _ple(h2, p.reshape(T, PLE_DIM), lw["w_ple"], lw["g_ple"], lw["w_ple_gate"], g_final)
    return y.reshape(B, L, D_MODEL), s_new, r_new, c_new


def kernel(x_prompt, x_sample, state_hgrn, state_rglru, state_conv, p_prompt, p_sample, g_mix, w_in, lower_bounds, g_hgrn_norm, conv_w, conv_b, w_rg_a, b_rg_a, w_rg_x, b_rg_x, lambda_rg, w_out, g_ffn, w_peer_q, peer_keys, peer_u, peer_v, g_ple, w_ple_gate, w_ple, g_final):
    depth = w_in.shape[0]
    assert depth == 1, "the final norm is fused into the last layer's kernel; depth 1 only"
    layer = 0
    lw = dict(g_mix=g_mix[layer], w_in=w_in[layer], g_hgrn_norm=g_hgrn_norm[layer],
              conv_w=conv_w[layer], conv_b=conv_b[layer], w_rg_a=w_rg_a[layer], b_rg_a=b_rg_a[layer],
              w_rg_x=w_rg_x[layer], b_rg_x=b_rg_x[layer], lambda_rg=lambda_rg[layer],
              w_out=w_out[layer], g_ffn=g_ffn[layer], w_peer_q=w_peer_q[layer],
              peer_keys=peer_keys[layer], peer_u=peer_u[layer], peer_v=peer_v[layer],
              g_ple=g_ple[layer], w_ple_gate=w_ple_gate[layer], w_ple=w_ple[layer])
    bp = x_prompt.shape[0]
    y_p, s_p, r_p, c_p = _run_stream(
        x_prompt, p_prompt[layer], jnp.zeros((bp, H_A, DK, DV), F32), jnp.zeros((bp, W_B), F32),
        jnp.zeros((bp, CONV_W - 1, W_B), F32), lw, g_final, lower_bounds, layer)
    y_s, s_s, r_s, c_s = _run_stream(
        x_sample, p_sample[layer], state_hgrn[layer], state_rglru[layer], state_conv[layer],
        lw, g_final, lower_bounds, layer)
    return (y_p, y_s, s_p[None], r_p[None], c_p[None], s_s[None], r_s[None], c_s[None])
```

```python
import functools
import math

import jax
import jax.numpy as jnp
from jax import lax
from jax.experimental import pallas as pl
from jax.experimental.pallas import tpu as pltpu

D_MODEL = 1024
W_A = 512
DK = 128
DV = 128
H_A = W_A // DK
W_B = 512
CONV_W = 4
RG_C = 8.0
IN_COLS = 4 * W_A + 2 * W_B
CHUNK = 64
PEER_HEADS = 8
N_SUBKEYS = 128
PEER_TOPK = 16
PEER_DQ = 256
PEER_DHALF = PEER_DQ // 2
PEER_SEL = PEER_HEADS * PEER_TOPK
PLE_DIM = 256
EPS = 1e-6

LANES = 128
SUBLANES = 8
CONV_PAD = SUBLANES
ROW_TILES = D_MODEL // LANES
MIB = 1024 * 1024
VMEM_BYTES_V7X = 64 * MIB
VMEM_MIN_REQUEST = 16 * MIB

F32 = jnp.float32
BF16 = jnp.bfloat16


def _block(n, target):
    b = min(n, target)
    while n % b:
        b -= 1
    return b


def _params(semantics, block_bytes, scratch_bytes=0):
    need = 3 * block_bytes + scratch_bytes
    limit = min(max(need, VMEM_MIN_REQUEST), VMEM_BYTES_V7X // 8 * 7)
    return pltpu.CompilerParams(dimension_semantics=semantics, vmem_limit_bytes=limit)


def _f32_bytes(*shapes):
    return sum(4 * math.prod(s) for s in shapes)


def _rms(x, g):
    ms = jnp.mean(x * x, axis=-1, keepdims=True)
    return x * lax.rsqrt(ms + EPS) * g


def _sigmoid(x):
    return 1.0 / (1.0 + jnp.exp(-x))


def _gelu_tanh(x):
    c = math.sqrt(2.0 / math.pi)
    return x * (0.5 * (1.0 + jnp.tanh(c * (x + 0.044715 * (x * x * x)))))


def _bdot(a, b):
    return jnp.dot(a.astype(BF16), b.astype(BF16), preferred_element_type=F32)


def _bdot_nt(a, b):
    return lax.dot_general(a.astype(BF16), b.astype(BF16), (((1,), (1,)), ((), ())),
                           preferred_element_type=F32)


def _bdot_tn(a, b):
    return lax.dot_general(a.astype(BF16), b.astype(BF16), (((0,), (0,)), ((), ())),
                           preferred_element_type=F32)


def _cumsum_rows(x):
    n = x.shape[0]
    rows = lax.broadcasted_iota(jnp.int32, x.shape, 0)
    s = 1
    while s < n:
        x = x + jnp.where(rows >= s, pltpu.roll(x, s, 0), 0.0)
        s *= 2
    return x


def _linear_scan_rows(a, b):
    n = a.shape[0]
    rows = lax.broadcasted_iota(jnp.int32, a.shape, 0)
    s = 1
    while s < n:
        keep = rows >= s
        b = jnp.where(keep, a * pltpu.roll(b, s, 0) + b, b)
        a = jnp.where(keep, a * pltpu.roll(a, s, 0), a)
        s *= 2
    return a, b


def _hgrn_chunk(zq, zf, v, zg, lb, gn, tril, st_scr):
    chunk = zq.shape[0]
    f = lb + (1.0 - lb) * _sigmoid(zf)
    k = 1.0 - f
    q = zq * _sigmoid(zq)
    cum = _cumsum_rows(jnp.log(f))
    qd = q * jnp.exp(cum)
    kd = k * jnp.exp(-cum)
    last = cum[chunk - 1:chunk, :]
    kdec = k * jnp.exp(last - cum)
    dec = jnp.exp(last)
    outs = []
    for h in range(H_A):
        sl = slice(h * DK, (h + 1) * DK)
        att = jnp.where(tril, _bdot_nt(qd[:, sl], kd[:, sl]), 0.0)
        st = st_scr[h]
        o = _bdot(att, v[:, sl]) + _bdot_nt(qd[:, sl], st)
        st_scr[h] = st * dec[:, sl] + _bdot_tn(v[:, sl], kdec[:, sl])
        zgh = zg[:, sl]
        outs.append(_rms(o, gn[:, sl]) * (zgh * _sigmoid(zgh)))
    return outs


def _mixer_kernel(h_ref, gm_ref, win_ref, lbw_ref, gn_ref, cw_ref, cb_ref, wa_ref, ba_ref, wx_ref,
                  bx_ref, lam_ref, wout_ref, gf_ref, wq_ref, s0_ref, buf0_ref, r0_ref,
                  h1_ref, xn_ref, q_ref, s_out_ref, rlast_ref, bufnew_ref,
                  z_scr, o_scr, st_scr, xp_scr, hc_scr, *, tb, chunk, layer):
    t = pl.program_id(1)
    hist = CONV_W - 1

    @pl.when(t == 0)
    def _():
        for h in range(H_A):
            st_scr[h] = s0_ref[0, h].T
        xp_scr[CONV_PAD - hist:CONV_PAD, :] = buf0_ref[0]
        hc_scr[...] = r0_ref[0]

    hin = h_ref[0]
    z_scr[...] = _bdot(_rms(hin, gm_ref[...]), win_ref[...])

    lbw = lbw_ref[...]
    ex = jnp.exp(lbw - jnp.max(lbw, axis=0, keepdims=True))
    sm = ex / jnp.sum(ex, axis=0, keepdims=True)
    lb = jnp.sum(sm[0:layer + 1, :], axis=0, keepdims=True)
    gn = gn_ref[...]
    tril = (lax.broadcasted_iota(jnp.int32, (chunk, chunk), 0)
            >= lax.broadcasted_iota(jnp.int32, (chunk, chunk), 1))
    for c in range(tb // chunk):
        rows = pl.ds(c * chunk, chunk)
        zs = [z_scr[rows, j * W_A:(j + 1) * W_A] for j in range(4)]
        outs = _hgrn_chunk(zs[0], zs[1], zs[2], zs[3], lb, gn, tril, st_scr)
        for h in range(H_A):
            o_scr[rows, h * DK:(h + 1) * DK] = outs[h]

    x = z_scr[:, 4 * W_A:4 * W_A + W_B]
    xp_scr[CONV_PAD:CONV_PAD + tb, :] = x
    cw = cw_ref[...]
    acc = xp_scr[CONV_PAD - hist:CONV_PAD - hist + tb, :] * cw[0:1, :]
    for j in range(1, CONV_W):
        acc = acc + xp_scr[CONV_PAD - hist + j:CONV_PAD - hist + j + tb, :] * cw[j:j + 1, :]
    xc = cb_ref[...] + acc
    tail = xp_scr[CONV_PAD + tb - hist:CONV_PAD + tb, :]
    xp_scr[CONV_PAD - hist:CONV_PAD, :] = tail

    r = _sigmoid(_bdot(xc, wa_ref[...]) + ba_ref[...])
    gi = _sigmoid(_bdot(xc, wx_ref[...]) + bx_ref[...])
    nl = -lam_ref[...]
    softplus = jnp.maximum(nl, 0.0) + jnp.log1p(jnp.exp(-jnp.abs(nl)))
    log_a = -RG_C * r * softplus
    a = jnp.exp(log_a)
    one_minus_a2 = -jnp.tanh(log_a) * (a * a + 1.0)
    b = jnp.sqrt(one_minus_a2) * (gi * xc)
    aa, bb = _linear_scan_rows(a, b)
    hseq = bb + aa * hc_scr[...]
    hc_scr[...] = hseq[tb - 1:tb, :]
    o_scr[:, W_A:W_A + W_B] = hseq * _gelu_tanh(z_scr[:, 4 * W_A + W_B:4 * W_A + 2 * W_B])

    h1 = hin + _bdot(o_scr[...], wout_ref[...])
    xn = _rms(h1, gf_ref[...])
    for s in range(ROW_TILES):
        h1_ref[0, :, s, :] = h1[:, s * LANES:(s + 1) * LANES]
        xn_ref[0, :, s, :] = xn[:, s * LANES:(s + 1) * LANES]
    q_ref[0] = _bdot(xn, wq_ref[...])

    @pl.when(t == pl.num_programs(1) - 1)
    def _():
        for h in range(H_A):
            s_out_ref[0, h] = st_scr[h].T
        rlast_ref[0] = hseq[tb - 1:tb, :]
        bufnew_ref[0] = tail


def _block_diag(w):
    n, c, d = w.shape
    eye = jnp.eye(n, dtype=w.dtype)
    return (eye[:, None, :, None] * w[:, :, None, :]).reshape(n * c, n * d)


def _mixer(h, lw, lower_bounds, s0, buf0, r0, layer):
    B, L, _ = h.shape
    chunk = min(CHUNK, L)
    tb = chunk * _block(L // chunk, 4)
    nq = PEER_HEADS * PEER_DQ
    row = lambda v, w: v.reshape(1, w)
    full = lambda shape: pl.BlockSpec(shape, lambda b, t: (0,) * len(shape))
    per_batch = lambda shape: pl.BlockSpec((1,) + shape, lambda b, t: (b,) + (0,) * len(shape))
    tok3 = pl.BlockSpec((1, tb, ROW_TILES, LANES), lambda b, t: (b, t, 0, 0))
    kern = functools.partial(_mixer_kernel, tb=tb, chunk=chunk, layer=layer)
    h1, xn, q, s_new, r_new, c_new = pl.pallas_call(
        kern,
        out_shape=(jax.ShapeDtypeStruct((B, L, ROW_TILES, LANES), F32),
                   jax.ShapeDtypeStruct((B, L, ROW_TILES, LANES), F32),
                   jax.ShapeDtypeStruct((B, L, nq), F32),
                   jax.ShapeDtypeStruct((B, H_A, DK, DV), F32),
                   jax.ShapeDtypeStruct((B, 1, W_B), F32),
                   jax.ShapeDtypeStruct((B, CONV_W - 1, W_B), F32)),
        grid=(B, L // tb),
        in_specs=[pl.BlockSpec((1, tb, D_MODEL), lambda b, t: (b, t, 0)),
                  full((1, D_MODEL)), full((D_MODEL, IN_COLS)),
                  full(lower_bounds.shape), full((1, W_A)),
                  full((CONV_W, W_B)), full((1, W_B)),
                  full((W_B, W_B)), full((1, W_B)), full((W_B, W_B)), full((1, W_B)), full((1, W_B)),
                  full((W_A + W_B, D_MODEL)), full((1, D_MODEL)), full((D_MODEL, nq)),
                  per_batch((H_A, DK, DV)), per_batch((CONV_W - 1, W_B)), per_batch((1, W_B))],
        out_specs=(tok3, tok3,
                   pl.BlockSpec((1, tb, nq), lambda b, t: (b, t, 0)),
                   per_batch((H_A, DK, DV)), per_batch((1, W_B)), per_batch((CONV_W - 1, W_B))),
        scratch_shapes=[pltpu.VMEM((tb, IN_COLS), F32), pltpu.VMEM((tb, W_A + W_B), F32),
                        pltpu.VMEM((H_A, DV, DK), F32),
                        pltpu.VMEM((CONV_PAD + tb, W_B), F32), pltpu.VMEM((1, W_B), F32)],
        compiler_params=_params(
            ("arbitrary", "arbitrary"),
            _f32_bytes((tb, D_MODEL)) * 3 + _f32_bytes((tb, nq))
            + 2 * (D_MODEL * IN_COLS + 2 * W_B * W_B + (W_A + W_B) * D_MODEL + D_MODEL * nq),
            _f32_bytes((tb, IN_COLS), (tb, W_A + W_B), (H_A, DV, DK), (CONV_PAD + tb, W_B))),
        name="mixer",
    )(h, row(lw["g_mix"], D_MODEL), lw["w_in"].astype(BF16), lower_bounds,
      row(lw["g_hgrn_norm"], W_A), lw["conv_w"], row(lw["conv_b"], W_B),
      _block_diag(lw["w_rg_a"]).astype(BF16), row(lw["b_rg_a"], W_B),
      _block_diag(lw["w_rg_x"]).astype(BF16), row(lw["b_rg_x"], W_B), row(lw["lambda_rg"], W_B),
      lw["w_out"].astype(BF16), row(lw["g_ffn"], D_MODEL), lw["w_peer_q"].astype(BF16),
      s0, buf0, r0.reshape(B, 1, W_B))
    T = B * L
    return (h1.reshape(T, ROW_TILES, LANES), xn.reshape(T, ROW_TILES, LANES), q.reshape(T, nq),
            s_new, r_new.reshape(B, W_B), c_new)


def _topk_rows(vals, k, payload=None):
    n, w = vals.shape
    rows = lax.broadcasted_iota(jnp.int32, (n, w), 0).astype(F32)
    slot = lax.broadcasted_iota(jnp.int32, (k, w), 0)
    out_v = jnp.zeros((k, w), F32)
    out_p = jnp.zeros((k, w), F32)
    for i in range(k):
        m = jnp.max(vals, axis=0, keepdims=True)
        hit = jnp.min(jnp.where(vals == m, rows, float(n)), axis=0, keepdims=True)
        sel = rows == hit
        if payload is None:
            pick = hit
        else:
            pick = jnp.max(jnp.where(sel, payload, -1.0), axis=0, keepdims=True)
        out_v = jnp.where(slot == i, m, out_v)
        out_p = jnp.where(slot == i, pick, out_p)
        vals = jnp.where(sel, -jnp.inf, vals)
    return out_v, out_p


def _candidates(sv1, si1, sv2, si2):
    k, w = sv1.shape
    assert k == PEER_TOPK == 2 * SUBLANES
    row8 = lax.broadcasted_iota(jnp.int32, (SUBLANES, w), 0)
    vals = [sv1[0:1, :] + sv2, sv1[1:2, :] + sv2[0:SUBLANES, :]]
    idxs = [si1[0:1, :] * float(N_SUBKEYS) + si2, si1[1:2, :] * float(N_SUBKEYS) + si2[0:SUBLANES, :]]
    for i in range(2, SUBLANES):
        live = row8 < (k // (i + 1))
        vals.append(jnp.where(live, sv1[i:i + 1, :] + sv2[0:SUBLANES, :], -jnp.inf))
        idxs.append(si1[i:i + 1, :] * float(N_SUBKEYS) + si2[0:SUBLANES, :])
    vals.append(sv1[SUBLANES:k, :] + sv2[0:1, :])
    idxs.append(si1[SUBLANES:k, :] * float(N_SUBKEYS) + si2[0:1, :])
    return jnp.concatenate(vals, axis=0), jnp.concatenate(idxs, axis=0)


def _route_kernel(q_ref, keys_ref, idx_ref, gate_ref):
    def head(h, carry):
        col = pl.multiple_of(h * PEER_DQ, PEER_DQ)
        svs, sis = [], []
        for p in range(2):
            qs = q_ref[:, pl.ds(col + p * PEER_DHALF, PEER_DHALF)]
            s = _bdot_nt(keys_ref[h, p], qs)
            sv, si = _topk_rows(s, PEER_TOPK)
            svs.append(sv)
            sis.append(si)
        cand, cidx = _candidates(svs[0], sis[0], svs[1], sis[1])
        cs, ce = _topk_rows(cand, PEER_TOPK, payload=cidx)
        ex = jnp.exp(cs - cs[0:1, :])
        g = ex / jnp.sum(ex, axis=0, keepdims=True)
        row = pl.multiple_of(h * PEER_TOPK, PEER_TOPK)
        idx_ref[pl.ds(row, PEER_TOPK), :] = ce.astype(jnp.int32)
        gate_ref[pl.ds(row, PEER_TOPK), :] = g
        return carry

    lax.fori_loop(0, PEER_HEADS, head, 0)


def _route(q, keys):
    T = q.shape[0]
    tb = _block(T, 4 * LANES)
    nq = PEER_HEADS * PEER_DQ
    return pl.pallas_call(
        _route_kernel,
        out_shape=(jax.ShapeDtypeStruct((PEER_SEL, T), jnp.int32),
                   jax.ShapeDtypeStruct((PEER_SEL, T), F32)),
        grid=(T // tb,),
        in_specs=[pl.BlockSpec((tb, nq), lambda i: (i, 0)),
                  pl.BlockSpec(keys.shape, lambda i: (0, 0, 0, 0))],
        out_specs=(pl.BlockSpec((PEER_SEL, tb), lambda i: (0, i)),
                   pl.BlockSpec((PEER_SEL, tb), lambda i: (0, i))),
        compiler_params=_params(
            ("arbitrary",), _f32_bytes((tb, nq), (2 * PEER_SEL, tb)) + 2 * math.prod(keys.shape)),
        name="route",
    )(q, keys.astype(BF16))


GATHER_TOKENS = 64
GATHER_SLOTS = 8
DMA_PRIORITIES = 2


ACC_CHAINS = 2
ISSUE_LAG = 5


def _sum_sublanes_of_each(ps):
    assert len(ps) == SUBLANES == 8
    sub = lax.broadcasted_iota(jnp.int32, ps[0].shape, 0)
    low = sub < 4
    r = []
    for j in range(4):
        a, b = ps[j], ps[j + 4]
        r.append(jnp.where(low, a, b) + pltpu.roll(jnp.where(low, b, a), 4, 0))
    even2 = (sub & 2) == 0
    z = [jnp.where(even2, r[j] + pltpu.roll(r[j], 6, 0), r[j + 2] + pltpu.roll(r[j + 2], 2, 0))
         for j in range(2)]
    even1 = (sub & 1) == 0
    return jnp.where(even1, z[0] + pltpu.roll(z[0], 7, 0), z[1] + pltpu.roll(z[1], 1, 0))


def _gather_kernel(idx_ref, idx_next_ref, xn_ref, h1_ref, gate_ref, uv_hbm, out_ref, uvbuf, sem,
                   *, tbg, steps_per_gate_block):
    step = pl.program_id(0)
    n_steps = pl.num_programs(0)
    lane0 = (step % steps_per_gate_block) * tbg
    ahead = GATHER_SLOTS - 1

    def issue(ids_ref, base, slot, k_lo=0, k_hi=PEER_SEL):
        ids = ids_ref.at[0, 0, pl.ds(base, PEER_SEL)]
        for k in range(k_lo, k_hi):
            pltpu.make_async_copy(uv_hbm.at[ids[k]], uvbuf.at[slot, k],
                                  sem.at[slot]).start(priority=k % DMA_PRIORITIES)

    def wait(slot):
        pltpu.make_async_copy(uvbuf.at[slot], uvbuf.at[slot], sem.at[slot]).wait()

    lanes = lax.broadcasted_iota(jnp.int32, (SUBLANES, gate_ref.shape[1]), 1)

    def token(tok, ids_ref, ids_base):
        slot = tok & (GATHER_SLOTS - 1)
        slot_ahead = (tok + ahead) & (GATHER_SLOTS - 1)
        wait(slot)
        x8 = xn_ref[tok]
        lane = lane0 + tok
        accs = [None] * ACC_CHAINS
        n_grp = PEER_SEL // SUBLANES
        for grp in range(n_grp):
            k0 = grp * SUBLANES
            prods = [uvbuf[slot, k0 + j, 0:ROW_TILES, :] * x8 for j in range(SUBLANES)]
            g8 = jnp.sum(jnp.where(lanes == lane, gate_ref[k0:k0 + SUBLANES, :], 0.0),
                         axis=1, keepdims=True)
            chunks = [grp - ISSUE_LAG] if grp < n_grp - 1 else range(grp - ISSUE_LAG, n_grp)
            for c in chunks:
                if c >= 0:
                    issue(ids_ref, ids_base, slot_ahead, c * SUBLANES, (c + 1) * SUBLANES)
            act = jnp.sum(_sum_sublanes_of_each(prods), axis=1, keepdims=True)
            w = jnp.broadcast_to(g8 * _gelu_tanh(act), (SUBLANES, LANES))
            for j in range(SUBLANES):
                term = (jnp.broadcast_to(w[j:j + 1, :], (ROW_TILES, LANES))
                        * uvbuf[slot, k0 + j, ROW_TILES:2 * ROW_TILES, :])
                c = (k0 + j) % ACC_CHAINS
                accs[c] = term if accs[c] is None else accs[c] + term
        out_ref[tok] = h1_ref[tok] + functools.reduce(lambda a, b: a + b, accs)

    @pl.when(step == 0)
    def _():
        for s in range(ahead):
            issue(idx_ref, s * PEER_SEL, s)

    def body_same_step(tok, carry):
        token(tok, idx_ref, (tok + ahead) * PEER_SEL)
        return carry

    def body_next_step(tok, carry):
        token(tok, idx_next_ref, (tok + ahead - tbg) * PEER_SEL)
        return carry

    lax.fori_loop(0, tbg - ahead, body_same_step, 0)
    lax.fori_loop(tbg - ahead, tbg, body_next_step, 0)

    @pl.when(step == n_steps - 1)
    def _():
        for s in range(ahead):
            wait(s)


def _gather(eidx, gates, xn, h1, peer_u, peer_v):
    T = xn.shape[0]
    tbg = _block(T, GATHER_TOKENS)
    assert tbg % GATHER_SLOTS == 0 and GATHER_SLOTS & (GATHER_SLOTS - 1) == 0
    n_steps = T // tbg
    ahead = GATHER_SLOTS - 1
    gate_tb = _block(T, LANES)
    per = gate_tb // tbg
    n_exp = peer_u.shape[0]
    uv = jnp.concatenate([peer_u.reshape(n_exp, ROW_TILES, LANES),
                          peer_v.reshape(n_exp, ROW_TILES, LANES)], axis=1)
    idx = eidx.T.reshape(n_steps, 1, tbg * PEER_SEL)
    idx_head = idx[:, :, :ahead * PEER_SEL]
    kern = functools.partial(_gather_kernel, tbg=tbg, steps_per_gate_block=per)
    tok_spec = pl.BlockSpec((tbg, ROW_TILES, LANES), lambda i: (i, 0, 0))
    out = pl.pallas_call(
        kern,
        out_shape=jax.ShapeDtypeStruct((T, ROW_TILES, LANES), F32),
        grid=(n_steps,),
        in_specs=[pl.BlockSpec((1, 1, tbg * PEER_SEL), lambda i: (i, 0, 0),
                               memory_space=pltpu.SMEM),
                  pl.BlockSpec((1, 1, ahead * PEER_SEL),
                               lambda i: (jnp.minimum(i + 1, n_steps - 1), 0, 0),
                               memory_space=pltpu.SMEM),
                  tok_spec, tok_spec,
                  pl.BlockSpec((PEER_SEL, gate_tb), lambda i: (0, i // per)),
                  pl.BlockSpec(memory_space=pl.ANY)],
        out_specs=tok_spec,
        scratch_shapes=[pltpu.VMEM((GATHER_SLOTS, PEER_SEL, 2 * ROW_TILES, LANES), F32),
                        pltpu.SemaphoreType.DMA((GATHER_SLOTS,))],
        compiler_params=_params(
            ("arbitrary",), _f32_bytes((tbg, D_MODEL)) * 3 + _f32_bytes((PEER_SEL, gate_tb)),
            _f32_bytes((GATHER_SLOTS, PEER_SEL, 2 * D_MODEL))),
        name="gather",
    )(idx, idx_head, xn, h1, gates, uv)
    return out


def _ple_kernel(h_ref, p_ref, wp_ref, gp_ref, wg_ref, gf_ref, y_ref):
    h2 = jnp.concatenate([h_ref[:, s, :] for s in range(ROW_TILES)], axis=1)
    gate = _sigmoid(_bdot(_rms(h2, gp_ref[...]), wg_ref[...]))
    h3 = h2 + _bdot(p_ref[...], wp_ref[...]) * gate
    y_ref[...] = _rms(h3, gf_ref[...])


def _ple(h2, p2d, w_ple, g_ple, w_gate, g_final):
    T = h2.shape[0]
    tb = _block(T, 256)
    tok = lambda w: pl.BlockSpec((tb, w), lambda i: (i, 0))
    full = lambda shape: pl.BlockSpec(shape, lambda i: (0, 0))
    return pl.pallas_call(
        _ple_kernel,
        out_shape=jax.ShapeDtypeStruct((T, D_MODEL), F32),
        grid=(T // tb,),
        in_specs=[pl.BlockSpec((tb, ROW_TILES, LANES), lambda i: (i, 0, 0)),
                  tok(PLE_DIM), full((PLE_DIM, D_MODEL)), full((1, D_MODEL)),
                  full((D_MODEL, D_MODEL)), full((1, D_MODEL))],
        out_specs=tok(D_MODEL),
        compiler_params=_params(
            ("arbitrary",), _f32_bytes((tb, D_MODEL)) * 2 + _f32_bytes((tb, PLE_DIM))
            + 2 * (PLE_DIM * D_MODEL + D_MODEL * D_MODEL)),
        name="ple",
    )(h2, p2d, w_ple.astype(BF16), g_ple.reshape(1, D_MODEL), w_gate.astype(BF16),
      g_final.reshape(1, D_MODEL))


def _run_stream(h, p, s0, r0, c0, lw, g_final, lower_bounds, layer):
    B, L, _ = h.shape
    T = B * L
    h1, xn, q, s_new, r_new, c_new = _mixer(h, lw, lower_bounds, s0, c0, r0, layer)
    eidx, gates = _route(q, lw["peer_keys"])
    h2 = _gather(eidx, gates, xn, h1, lw["peer_u"], lw["peer_v"])
    y = _ple(h2, p.reshape(T, PLE_DIM), lw["w_ple"], lw["g_ple"], lw["w_ple_gate"], g_final)
    return y.reshape(B, L, D_MODEL), s_new, r_new, c_new


def kernel(x_prompt, x_sample, state_hgrn, state_rglru, state_conv, p_prompt, p_sample, g_mix, w_in, lower_bounds, g_hgrn_norm, conv_w, conv_b, w_rg_a, b_rg_a, w_rg_x, b_rg_x, lambda_rg, w_out, g_ffn, w_peer_q, peer_keys, peer_u, peer_v, g_ple, w_ple_gate, w_ple, g_final):
    depth = w_in.shape[0]
    assert depth == 1, "the final norm is fused into the last layer's kernel; depth 1 only"
    layer = 0
    lw = dict(g_mix=g_mix[layer], w_in=w_in[layer], g_hgrn_norm=g_hgrn_norm[layer],
              conv_w=conv_w[layer], conv_b=conv_b[layer], w_rg_a=w_rg_a[layer], b_rg_a=b_rg_a[layer],
              w_rg_x=w_rg_x[layer], b_rg_x=b_rg_x[layer], lambda_rg=lambda_rg[layer],
              w_out=w_out[layer], g_ffn=g_ffn[layer], w_peer_q=w_peer_q[layer],
              peer_keys=peer_keys[layer], peer_u=peer_u[layer], peer_v=peer_v[layer],
              g_ple=g_ple[layer], w_ple_gate=w_ple_gate[layer], w_ple=w_ple[layer])
    bp = x_prompt.shape[0]
    y_p, s_p, r_p, c_p = _run_stream(
        x_prompt, p_prompt[layer], jnp.zeros((bp, H_A, DK, DV), F32), jnp.zeros((bp, W_B), F32),
        jnp.zeros((bp, CONV_W - 1, W_B), F32), lw, g_final, lower_bounds, layer)
    y_s, s_s, r_s, c_s = _run_stream(
        x_sample, p_sample[layer], state_hgrn[layer], state_rglru[layer], state_conv[layer],
        lw, g_final, lower_bounds, layer)
    return (y_p, y_s, s_p[None], r_p[None], c_p[None], s_s[None], r_s[None], c_s[None])
```

```python
import functools
import math

import jax
import jax.numpy as jnp
from jax import lax
from jax.experimental import pallas as pl
from jax.experimental.pallas import tpu as pltpu

D_MODEL = 1024
W_A = 512
DK = 128
DV = 128
H_A = W_A // DK
W_B = 512
CONV_W = 4
RG_C = 8.0
IN_COLS = 4 * W_A + 2 * W_B
CHUNK = 64
PEER_HEADS = 8
N_SUBKEYS = 128
PEER_TOPK = 16
PEER_DQ = 256
PEER_DHALF = PEER_DQ // 2
PEER_SEL = PEER_HEADS * PEER_TOPK
PLE_DIM = 256
EPS = 1e-6

LANES = 128
SUBLANES = 8
CONV_PAD = SUBLANES
ROW_TILES = D_MODEL // LANES
MIB = 1024 * 1024
VMEM_BYTES_V7X = 64 * MIB
VMEM_MIN_REQUEST = 16 * MIB

F32 = jnp.float32
BF16 = jnp.bfloat16


def _block(n, target):
    b = min(n, target)
    while n % b:
        b -= 1
    return b


def _params(semantics, block_bytes, scratch_bytes=0):
    need = 3 * block_bytes + scratch_bytes
    limit = min(max(need, VMEM_MIN_REQUEST), VMEM_BYTES_V7X // 8 * 7)
    return pltpu.CompilerParams(dimension_semantics=semantics, vmem_limit_bytes=limit)


def _f32_bytes(*shapes):
    return sum(4 * math.prod(s) for s in shapes)


def _rms(x, g):
    ms = jnp.mean(x * x, axis=-1, keepdims=True)
    return x * lax.rsqrt(ms + EPS) * g


def _sigmoid(x):
    return 1.0 / (1.0 + jnp.exp(-x))


def _gelu_tanh(x):
    c = math.sqrt(2.0 / math.pi)
    return x * (0.5 * (1.0 + jnp.tanh(c * (x + 0.044715 * (x * x * x)))))


def _bdot(a, b):
    return jnp.dot(a.astype(BF16), b.astype(BF16), preferred_element_type=F32)


def _bdot_nt(a, b):
    return lax.dot_general(a.astype(BF16), b.astype(BF16), (((1,), (1,)), ((), ())),
                           preferred_element_type=F32)


def _bdot_tn(a, b):
    return lax.dot_general(a.astype(BF16), b.astype(BF16), (((0,), (0,)), ((), ())),
                           preferred_element_type=F32)


def _cumsum_rows(x):
    n = x.shape[0]
    rows = lax.broadcasted_iota(jnp.int32, x.shape, 0)
    s = 1
    while s < n:
        x = x + jnp.where(rows >= s, pltpu.roll(x, s, 0), 0.0)
        s *= 2
    return x


def _linear_scan_rows(a, b):
    n = a.shape[0]
    rows = lax.broadcasted_iota(jnp.int32, a.shape, 0)
    s = 1
    while s < n:
        keep = rows >= s
        b = jnp.where(keep, a * pltpu.roll(b, s, 0) + b, b)
        a = jnp.where(keep, a * pltpu.roll(a, s, 0), a)
        s *= 2
    return a, b


def _hgrn_chunk(zq, zf, v, zg, lb, gn, tril, st_scr):
    chunk = zq.shape[0]
    f = lb + (1.0 - lb) * _sigmoid(zf)
    k = 1.0 - f
    q = zq * _sigmoid(zq)
    cum = _cumsum_rows(jnp.log(f))
    qd = q * jnp.exp(cum)
    kd = k * jnp.exp(-cum)
    last = cum[chunk - 1:chunk, :]
    kdec = k * jnp.exp(last - cum)
    dec = jnp.exp(last)
    outs = []
    for h in range(H_A):
        sl = slice(h * DK, (h + 1) * DK)
        att = jnp.where(tril, _bdot_nt(qd[:, sl], kd[:, sl]), 0.0)
        st = st_scr[h]
        o = _bdot(att, v[:, sl]) + _bdot_nt(qd[:, sl], st)
        st_scr[h] = st * dec[:, sl] + _bdot_tn(v[:, sl], kdec[:, sl])
        zgh = zg[:, sl]
        outs.append(_rms(o, gn[:, sl]) * (zgh * _sigmoid(zgh)))
    return outs


def _mixer_kernel(h_ref, gm_ref, win_ref, lbw_ref, gn_ref, cw_ref, cb_ref, wa_ref, ba_ref, wx_ref,
                  bx_ref, lam_ref, wout_ref, gf_ref, wq_ref, s0_ref, buf0_ref, r0_ref,
                  h1_ref, xn_ref, q_ref, s_out_ref, rlast_ref, bufnew_ref,
                  z_scr, o_scr, st_scr, xp_scr, hc_scr, *, tb, chunk, layer):
    t = pl.program_id(1)
    hist = CONV_W - 1

    @pl.when(t == 0)
    def _():
        for h in range(H_A):
            st_scr[h] = s0_ref[0, h].T
        xp_scr[CONV_PAD - hist:CONV_PAD, :] = buf0_ref[0]
        hc_scr[...] = r0_ref[0]

    hin = h_ref[0]
    z_scr[...] = _bdot(_rms(hin, gm_ref[...]), win_ref[...])

    lbw = lbw_ref[...]
    ex = jnp.exp(lbw - jnp.max(lbw, axis=0, keepdims=True))
    sm = ex / jnp.sum(ex, axis=0, keepdims=True)
    lb = jnp.sum(sm[0:layer + 1, :], axis=0, keepdims=True)
    gn = gn_ref[...]
    tril = (lax.broadcasted_iota(jnp.int32, (chunk, chunk), 0)
            >= lax.broadcasted_iota(jnp.int32, (chunk, chunk), 1))
    for c in range(tb // chunk):
        rows = pl.ds(c * chunk, chunk)
        zs = [z_scr[rows, j * W_A:(j + 1) * W_A] for j in range(4)]
        outs = _hgrn_chunk(zs[0], zs[1], zs[2], zs[3], lb, gn, tril, st_scr)
        for h in range(H_A):
            o_scr[rows, h * DK:(h + 1) * DK] = outs[h]

    x = z_scr[:, 4 * W_A:4 * W_A + W_B]
    xp_scr[CONV_PAD:CONV_PAD + tb, :] = x
    cw = cw_ref[...]
    acc = xp_scr[CONV_PAD - hist:CONV_PAD - hist + tb, :] * cw[0:1, :]
    for j in range(1, CONV_W):
        acc = acc + xp_scr[CONV_PAD - hist + j:CONV_PAD - hist + j + tb, :] * cw[j:j + 1, :]
    xc = cb_ref[...] + acc
    tail = xp_scr[CONV_PAD + tb - hist:CONV_PAD + tb, :]
    xp_scr[CONV_PAD - hist:CONV_PAD, :] = tail

    r = _sigmoid(_bdot(xc, wa_ref[...]) + ba_ref[...])
    gi = _sigmoid(_bdot(xc, wx_ref[...]) + bx_ref[...])
    nl = -lam_ref[...]
    softplus = jnp.maximum(nl, 0.0) + jnp.log1p(jnp.exp(-jnp.abs(nl)))
    log_a = -RG_C * r * softplus
    a = jnp.exp(log_a)
    one_minus_a2 = -jnp.tanh(log_a) * (a * a + 1.0)
    b = jnp.sqrt(one_minus_a2) * (gi * xc)
    aa, bb = _linear_scan_rows(a, b)
    hseq = bb + aa * hc_scr[...]
    hc_scr[...] = hseq[tb - 1:tb, :]
    o_scr[:, W_A:W_A + W_B] = hseq * _gelu_tanh(z_scr[:, 4 * W_A + W_B:4 * W_A + 2 * W_B])

    h1 = hin + _bdot(o_scr[...], wout_ref[...])
    xn = _rms(h1, gf_ref[...])
    for s in range(ROW_TILES):
        h1_ref[0, :, s, :] = h1[:, s * LANES:(s + 1) * LANES]
        xn_ref[0, :, s, :] = xn[:, s * LANES:(s + 1) * LANES]
    q_ref[0] = _bdot(xn, wq_ref[...])

    @pl.when(t == pl.num_programs(1) - 1)
    def _():
        for h in range(H_A):
            s_out_ref[0, h] = st_scr[h].T
        rlast_ref[0] = hseq[tb - 1:tb, :]
        bufnew_ref[0] = tail


def _block_diag(w):
    n, c, d = w.shape
    eye = jnp.eye(n, dtype=w.dtype)
    return (eye[:, None, :, None] * w[:, :, None, :]).reshape(n * c, n * d)


def _mixer(h, lw, lower_bounds, s0, buf0, r0, layer):
    B, L, _ = h.shape
    chunk = min(CHUNK, L)
    tb = chunk * _block(L // chunk, 4)
    nq = PEER_HEADS * PEER_DQ
    row = lambda v, w: v.reshape(1, w)
    full = lambda shape: pl.BlockSpec(shape, lambda b, t: (0,) * len(shape))
    per_batch = lambda shape: pl.BlockSpec((1,) + shape, lambda b, t: (b,) + (0,) * len(shape))
    tok3 = pl.BlockSpec((1, tb, ROW_TILES, LANES), lambda b, t: (b, t, 0, 0))
    kern = functools.partial(_mixer_kernel, tb=tb, chunk=chunk, layer=layer)
    h1, xn, q, s_new, r_new, c_new = pl.pallas_call(
        kern,
        out_shape=(jax.ShapeDtypeStruct((B, L, ROW_TILES, LANES), F32),
                   jax.ShapeDtypeStruct((B, L, ROW_TILES, LANES), F32),
                   jax.ShapeDtypeStruct((B, L, nq), F32),
                   jax.ShapeDtypeStruct((B, H_A, DK, DV), F32),
                   jax.ShapeDtypeStruct((B, 1, W_B), F32),
                   jax.ShapeDtypeStruct((B, CONV_W - 1, W_B), F32)),
        grid=(B, L // tb),
        in_specs=[pl.BlockSpec((1, tb, D_MODEL), lambda b, t: (b, t, 0)),
                  full((1, D_MODEL)), full((D_MODEL, IN_COLS)),
                  full(lower_bounds.shape), full((1, W_A)),
                  full((CONV_W, W_B)), full((1, W_B)),
                  full((W_B, W_B)), full((1, W_B)), full((W_B, W_B)), full((1, W_B)), full((1, W_B)),
                  full((W_A + W_B, D_MODEL)), full((1, D_MODEL)), full((D_MODEL, nq)),
                  per_batch((H_A, DK, DV)), per_batch((CONV_W - 1, W_B)), per_batch((1, W_B))],
        out_specs=(tok3, tok3,
                   pl.BlockSpec((1, tb, nq), lambda b, t: (b, t, 0)),
                   per_batch((H_A, DK, DV)), per_batch((1, W_B)), per_batch((CONV_W - 1, W_B))),
        scratch_shapes=[pltpu.VMEM((tb, IN_COLS), F32), pltpu.VMEM((tb, W_A + W_B), F32),
                        pltpu.VMEM((H_A, DV, DK), F32),
                        pltpu.VMEM((CONV_PAD + tb, W_B), F32), pltpu.VMEM((1, W_B), F32)],
        compiler_params=_params(
            ("arbitrary", "arbitrary"),
            _f32_bytes((tb, D_MODEL)) * 3 + _f32_bytes((tb, nq))
            + 2 * (D_MODEL * IN_COLS + 2 * W_B * W_B + (W_A + W_B) * D_MODEL + D_MODEL * nq),
            _f32_bytes((tb, IN_COLS), (tb, W_A + W_B), (H_A, DV, DK), (CONV_PAD + tb, W_B))),
        name="mixer",
    )(h, row(lw["g_mix"], D_MODEL), lw["w_in"].astype(BF16), lower_bounds,
      row(lw["g_hgrn_norm"], W_A), lw["conv_w"], row(lw["conv_b"], W_B),
      _block_diag(lw["w_rg_a"]).astype(BF16), row(lw["b_rg_a"], W_B),
      _block_diag(lw["w_rg_x"]).astype(BF16), row(lw["b_rg_x"], W_B), row(lw["lambda_rg"], W_B),
      lw["w_out"].astype(BF16), row(lw["g_ffn"], D_MODEL), lw["w_peer_q"].astype(BF16),
      s0, buf0, r0.reshape(B, 1, W_B))
    T = B * L
    return (h1.reshape(T, ROW_TILES, LANES), xn.reshape(T, ROW_TILES, LANES), q.reshape(T, nq),
            s_new, r_new.reshape(B, W_B), c_new)


def _topk_rows(vals, k, payload=None):
    n, w = vals.shape
    rows = lax.broadcasted_iota(jnp.int32, (n, w), 0).astype(F32)
    slot = lax.broadcasted_iota(jnp.int32, (k, w), 0)
    out_v = jnp.zeros((k, w), F32)
    out_p = jnp.zeros((k, w), F32)
    for i in range(k):
        m = jnp.max(vals, axis=0, keepdims=True)
        hit = jnp.min(jnp.where(vals == m, rows, float(n)), axis=0, keepdims=True)
        sel = rows == hit
        if payload is None:
            pick = hit
        else:
            pick = jnp.max(jnp.where(sel, payload, -1.0), axis=0, keepdims=True)
        out_v = jnp.where(slot == i, m, out_v)
        out_p = jnp.where(slot == i, pick, out_p)
        vals = jnp.where(sel, -jnp.inf, vals)
    return out_v, out_p


def _candidates(sv1, si1, sv2, si2):
    k, w = sv1.shape
    assert k == PEER_TOPK == 2 * SUBLANES
    row8 = lax.broadcasted_iota(jnp.int32, (SUBLANES, w), 0)
    vals = [sv1[0:1, :] + sv2, sv1[1:2, :] + sv2[0:SUBLANES, :]]
    idxs = [si1[0:1, :] * float(N_SUBKEYS) + si2, si1[1:2, :] * float(N_SUBKEYS) + si2[0:SUBLANES, :]]
    for i in range(2, SUBLANES):
        live = row8 < (k // (i + 1))
        vals.append(jnp.where(live, sv1[i:i + 1, :] + sv2[0:SUBLANES, :], -jnp.inf))
        idxs.append(si1[i:i + 1, :] * float(N_SUBKEYS) + si2[0:SUBLANES, :])
    vals.append(sv1[SUBLANES:k, :] + sv2[0:1, :])
    idxs.append(si1[SUBLANES:k, :] * float(N_SUBKEYS) + si2[0:1, :])
    return jnp.concatenate(vals, axis=0), jnp.concatenate(idxs, axis=0)


def _route_kernel(q_ref, keys_ref, idx_ref, gate_ref):
    def head(h, carry):
        col = pl.multiple_of(h * PEER_DQ, PEER_DQ)
        svs, sis = [], []
        for p in range(2):
            qs = q_ref[:, pl.ds(col + p * PEER_DHALF, PEER_DHALF)]
            s = _bdot_nt(keys_ref[h, p], qs)
            sv, si = _topk_rows(s, PEER_TOPK)
            svs.append(sv)
            sis.append(si)
        cand, cidx = _candidates(svs[0], sis[0], svs[1], sis[1])
        cs, ce = _topk_rows(cand, PEER_TOPK, payload=cidx)
        ex = jnp.exp(cs - cs[0:1, :])
        g = ex / jnp.sum(ex, axis=0, keepdims=True)
        row = pl.multiple_of(h * PEER_TOPK, PEER_TOPK)
        idx_ref[pl.ds(row, PEER_TOPK), :] = ce.astype(jnp.int32)
        gate_ref[pl.ds(row, PEER_TOPK), :] = g
        return carry

    lax.fori_loop(0, PEER_HEADS, head, 0)


def _route(q, keys):
    T = q.shape[0]
    tb = _block(T, 4 * LANES)
    nq = PEER_HEADS * PEER_DQ
    return pl.pallas_call(
        _route_kernel,
        out_shape=(jax.ShapeDtypeStruct((PEER_SEL, T), jnp.int32),
                   jax.ShapeDtypeStruct((PEER_SEL, T), F32)),
        grid=(T // tb,),
        in_specs=[pl.BlockSpec((tb, nq), lambda i: (i, 0)),
                  pl.BlockSpec(keys.shape, lambda i: (0, 0, 0, 0))],
        out_specs=(pl.BlockSpec((PEER_SEL, tb), lambda i: (0, i)),
                   pl.BlockSpec((PEER_SEL, tb), lambda i: (0, i))),
        compiler_params=_params(
            ("arbitrary",), _f32_bytes((tb, nq), (2 * PEER_SEL, tb)) + 2 * math.prod(keys.shape)),
        name="route",
    )(q, keys.astype(BF16))


GATHER_TOKENS = 64
GATHER_SLOTS = 8
DMA_PRIORITIES = 2


ACC_CHAINS = 2
ISSUE_LAG = 6


def _sum_sublanes_of_each(ps):
    assert len(ps) == SUBLANES == 8
    sub = lax.broadcasted_iota(jnp.int32, ps[0].shape, 0)
    low = sub < 4
    r = []
    for j in range(4):
        a, b = ps[j], ps[j + 4]
        r.append(jnp.where(low, a, b) + pltpu.roll(jnp.where(low, b, a), 4, 0))
    even2 = (sub & 2) == 0
    z = [jnp.where(even2, r[j] + pltpu.roll(r[j], 6, 0), r[j + 2] + pltpu.roll(r[j + 2], 2, 0))
         for j in range(2)]
    even1 = (sub & 1) == 0
    return jnp.where(even1, z[0] + pltpu.roll(z[0], 7, 0), z[1] + pltpu.roll(z[1], 1, 0))


def _gather_kernel(idx_ref, idx_next_ref, xn_ref, h1_ref, gate_ref, uv_hbm, out_ref, uvbuf, sem,
                   *, tbg, steps_per_gate_block):
    step = pl.program_id(0)
    n_steps = pl.num_programs(0)
    lane0 = (step % steps_per_gate_block) * tbg
    ahead = GATHER_SLOTS - 1

    def issue(ids_ref, base, slot, k_lo=0, k_hi=PEER_SEL):
        ids = ids_ref.at[0, 0, pl.ds(base, PEER_SEL)]
        for k in range(k_lo, k_hi):
            pltpu.make_async_copy(uv_hbm.at[ids[k]], uvbuf.at[slot, k],
                                  sem.at[slot]).start(priority=k % DMA_PRIORITIES)

    def wait(slot):
        pltpu.make_async_copy(uvbuf.at[slot], uvbuf.at[slot], sem.at[slot]).wait()

    lanes = lax.broadcasted_iota(jnp.int32, (SUBLANES, gate_ref.shape[1]), 1)

    def token(tok, ids_ref, ids_base):
        slot = tok & (GATHER_SLOTS - 1)
        slot_ahead = (tok + ahead) & (GATHER_SLOTS - 1)
        wait(slot)
        x8 = xn_ref[tok]
        lane = lane0 + tok
        accs = [None] * ACC_CHAINS
        n_grp = PEER_SEL // SUBLANES
        for grp in range(n_grp):
            k0 = grp * SUBLANES
            prods = [uvbuf[slot, k0 + j].astype(F32)[0:ROW_TILES, :] * x8 for j in range(SUBLANES)]
            g8 = jnp.sum(jnp.where(lanes == lane, gate_ref[k0:k0 + SUBLANES, :], 0.0),
                         axis=1, keepdims=True)
            chunks = [grp - ISSUE_LAG] if grp < n_grp - 1 else range(grp - ISSUE_LAG, n_grp)
            for c in chunks:
                if c >= 0:
                    issue(ids_ref, ids_base, slot_ahead, c * SUBLANES, (c + 1) * SUBLANES)
            act = jnp.sum(_sum_sublanes_of_each(prods), axis=1, keepdims=True)
            w = jnp.broadcast_to(g8 * _gelu_tanh(act), (SUBLANES, LANES))
            for j in range(SUBLANES):
                term = (jnp.broadcast_to(w[j:j + 1, :], (ROW_TILES, LANES))
                        * uvbuf[slot, k0 + j].astype(F32)[ROW_TILES:2 * ROW_TILES, :])
                c = (k0 + j) % ACC_CHAINS
                accs[c] = term if accs[c] is None else accs[c] + term
        out_ref[tok] = h1_ref[tok] + functools.reduce(lambda a, b: a + b, accs)

    @pl.when(step == 0)
    def _():
        for s in range(ahead):
            issue(idx_ref, s * PEER_SEL, s)

    def body_same_step(tok, carry):
        token(tok, idx_ref, (tok + ahead) * PEER_SEL)
        return carry

    def body_next_step(tok, carry):
        token(tok, idx_next_ref, (tok + ahead - tbg) * PEER_SEL)
        return carry

    lax.fori_loop(0, tbg - ahead, body_same_step, 0)
    lax.fori_loop(tbg - ahead, tbg, body_next_step, 0)

    @pl.when(step == n_steps - 1)
    def _():
        for s in range(ahead):
            wait(s)


def _gather(eidx, gates, xn, h1, peer_u, peer_v):
    T = xn.shape[0]
    tbg = _block(T, GATHER_TOKENS)
    assert tbg % GATHER_SLOTS == 0 and GATHER_SLOTS & (GATHER_SLOTS - 1) == 0
    n_steps = T // tbg
    ahead = GATHER_SLOTS - 1
    gate_tb = _block(T, LANES)
    per = gate_tb // tbg
    n_exp = peer_u.shape[0]
    uv = jnp.concatenate([peer_u.reshape(n_exp, ROW_TILES, LANES),
                          peer_v.reshape(n_exp, ROW_TILES, LANES)], axis=1).astype(BF16)
    idx = eidx.T.reshape(n_steps, 1, tbg * PEER_SEL)
    idx_head = idx[:, :, :ahead * PEER_SEL]
    kern = functools.partial(_gather_kernel, tbg=tbg, steps_per_gate_block=per)
    tok_spec = pl.BlockSpec((tbg, ROW_TILES, LANES), lambda i: (i, 0, 0))
    out = pl.pallas_call(
        kern,
        out_shape=jax.ShapeDtypeStruct((T, ROW_TILES, LANES), F32),
        grid=(n_steps,),
        in_specs=[pl.BlockSpec((1, 1, tbg * PEER_SEL), lambda i: (i, 0, 0),
                               memory_space=pltpu.SMEM),
                  pl.BlockSpec((1, 1, ahead * PEER_SEL),
                               lambda i: (jnp.minimum(i + 1, n_steps - 1), 0, 0),
                               memory_space=pltpu.SMEM),
                  tok_spec, tok_spec,
                  pl.BlockSpec((PEER_SEL, gate_tb), lambda i: (0, i // per)),
                  pl.BlockSpec(memory_space=pl.ANY)],
        out_specs=tok_spec,
        scratch_shapes=[pltpu.VMEM((GATHER_SLOTS, PEER_SEL, 2 * ROW_TILES, LANES), BF16),
                        pltpu.SemaphoreType.DMA((GATHER_SLOTS,))],
        compiler_params=_params(
            ("arbitrary",), _f32_bytes((tbg, D_MODEL)) * 3 + _f32_bytes((PEER_SEL, gate_tb)),
            _f32_bytes((GATHER_SLOTS, PEER_SEL, D_MODEL))),
        name="gather",
    )(idx, idx_head, xn, h1, gates, uv)
    return out


def _ple_kernel(h_ref, p_ref, wp_ref, gp_ref, wg_ref, gf_ref, y_ref):
    h2 = jnp.concatenate([h_ref[:, s, :] for s in range(ROW_TILES)], axis=1)
    gate = _sigmoid(_bdot(_rms(h2, gp_ref[...]), wg_ref[...]))
    h3 = h2 + _bdot(p_ref[...], wp_ref[...]) * gate
    y_ref[...] = _rms(h3, gf_ref[...])


def _ple(h2, p2d, w_ple, g_ple, w_gate, g_final):
    T = h2.shape[0]
    tb = _block(T, 256)
    tok = lambda w: pl.BlockSpec((tb, w), lambda i: (i, 0))
    full = lambda shape: pl.BlockSpec(shape, lambda i: (0, 0))
    return pl.pallas_call(
        _ple_kernel,
        out_shape=jax.ShapeDtypeStruct((T, D_MODEL), F32),
        grid=(T // tb,),
        in_specs=[pl.BlockSpec((tb, ROW_TILES, LANES), lambda i: (i, 0, 0)),
                  tok(PLE_DIM), full((PLE_DIM, D_MODEL)), full((1, D_MODEL)),
                  full((D_MODEL, D_MODEL)), full((1, D_MODEL))],
        out_specs=tok(D_MODEL),
        compiler_params=_params(
            ("arbitrary",), _f32_bytes((tb, D_MODEL)) * 2 + _f32_bytes((tb, PLE_DIM))
            + 2 * (PLE_DIM * D_MODEL + D_MODEL * D_MODEL)),
        name="ple",
    )(h2, p2d, w_ple.astype(BF16), g_ple.reshape(1, D_MODEL), w_gate.astype(BF16),
      g_final.reshape(1, D_MODEL))


def _run_stream(h, p, s0, r0, c0, lw, g_final, lower_bounds, layer):
    B, L, _ = h.shape
    T = B * L
    h1, xn, q, s_new, r_new, c_new = _mixer(h, lw, lower_bounds, s0, c0, r0, layer)
    eidx, gates = _route(q, lw["peer_keys"])
    h2 = _gather(eidx, gates, xn, h1, lw["peer_u"], lw["peer_v"])
    y = _ple(h2, p.reshape(T, PLE_DIM), lw["w_ple"], lw["g_ple"], lw["w_ple_gate"], g_final)
    return y.reshape(B, L, D_MODEL), s_new, r_new, c_new


def kernel(x_prompt, x_sample, state_hgrn, state_rglru, state_conv, p_prompt, p_sample, g_mix, w_in, lower_bounds, g_hgrn_norm, conv_w, conv_b, w_rg_a, b_rg_a, w_rg_x, b_rg_x, lambda_rg, w_out, g_ffn, w_peer_q, peer_keys, peer_u, peer_v, g_ple, w_ple_gate, w_ple, g_final):
    depth = w_in.shape[0]
    assert depth == 1, "the final norm is fused into the last layer's kernel; depth 1 only"
    layer = 0
    lw = dict(g_mix=g_mix[layer], w_in=w_in[layer], g_hgrn_norm=g_hgrn_norm[layer],
              conv_w=conv_w[layer], conv_b=conv_b[layer], w_rg_a=w_rg_a[layer], b_rg_a=b_rg_a[layer],
              w_rg_x=w_rg_x[layer], b_rg_x=b_rg_x[layer], lambda_rg=lambda_rg[layer],
              w_out=w_out[layer], g_ffn=g_ffn[layer], w_peer_q=w_peer_q[layer],
              peer_keys=peer_keys[layer], peer_u=peer_u[layer], peer_v=peer_v[layer],
              g_ple=g_ple[layer], w_ple_gate=w_ple_gate[layer], w_ple=w_ple[layer])
    bp = x_prompt.shape[0]
    y_p, s_p, r_p, c_p = _run_stream(
        x_prompt, p_prompt[layer], jnp.zeros((bp, H_A, DK, DV), F32), jnp.zeros((bp, W_B), F32),
        jnp.zeros((bp, CONV_W - 1, W_B), F32), lw, g_final, lower_bounds, layer)
    y_s, s_s, r_s, c_s = _run_stream(
        x_sample, p_sample[layer], state_hgrn[layer], state_rglru[layer], state_conv[layer],
        lw, g_final, lower_bounds, layer)
    return (y_p, y_s, s_p[None], r_p[None], c_p[None], s_s[None], r_s[None], c_s[None])
```

```python
import functools
import math

import jax
import jax.numpy as jnp
from jax import lax
from jax.experimental import pallas as pl
from jax.experimental.pallas import tpu as pltpu

D_MODEL = 1024
W_A = 512
DK = 128
DV = 128
H_A = W_A // DK
W_B = 512
CONV_W = 4
RG_C = 8.0
IN_COLS = 4 * W_A + 2 * W_B
CHUNK = 64
PEER_HEADS = 8
N_SUBKEYS = 128
PEER_TOPK = 16
PEER_DQ = 256
PEER_DHALF = PEER_DQ // 2
PEER_SEL = PEER_HEADS * PEER_TOPK
PLE_DIM = 256
EPS = 1e-6

LANES = 128
SUBLANES = 8
CONV_PAD = SUBLANES
ROW_TILES = D_MODEL // LANES
MIB = 1024 * 1024
VMEM_BYTES_V7X = 64 * MIB
VMEM_MIN_REQUEST = 16 * MIB

F32 = jnp.float32
BF16 = jnp.bfloat16


def _block(n, target):
    b = min(n, target)
    while n % b:
        b -= 1
    return b


def _params(semantics, block_bytes, scratch_bytes=0):
    need = 3 * block_bytes + scratch_bytes
    limit = min(max(need, VMEM_MIN_REQUEST), VMEM_BYTES_V7X // 8 * 7)
    return pltpu.CompilerParams(dimension_semantics=semantics, vmem_limit_bytes=limit)


def _f32_bytes(*shapes):
    return sum(4 * math.prod(s) for s in shapes)


def _rms(x, g):
    ms = jnp.mean(x * x, axis=-1, keepdims=True)
    return x * lax.rsqrt(ms + EPS) * g


def _sigmoid(x):
    return 1.0 / (1.0 + jnp.exp(-x))


def _gelu_tanh(x):
    c = math.sqrt(2.0 / math.pi)
    return x * (0.5 * (1.0 + jnp.tanh(c * (x + 0.044715 * (x * x * x)))))


def _bdot(a, b):
    return jnp.dot(a.astype(BF16), b.astype(BF16), preferred_element_type=F32)


def _bdot_nt(a, b):
    return lax.dot_general(a.astype(BF16), b.astype(BF16), (((1,), (1,)), ((), ())),
                           preferred_element_type=F32)


def _bdot_tn(a, b):
    return lax.dot_general(a.astype(BF16), b.astype(BF16), (((0,), (0,)), ((), ())),
                           preferred_element_type=F32)


def _cumsum_rows(x):
    n = x.shape[0]
    rows = lax.broadcasted_iota(jnp.int32, x.shape, 0)
    s = 1
    while s < n:
        x = x + jnp.where(rows >= s, pltpu.roll(x, s, 0), 0.0)
        s *= 2
    return x


def _linear_scan_rows(a, b):
    n = a.shape[0]
    rows = lax.broadcasted_iota(jnp.int32, a.shape, 0)
    s = 1
    while s < n:
        keep = rows >= s
        b = jnp.where(keep, a * pltpu.roll(b, s, 0) + b, b)
        a = jnp.where(keep, a * pltpu.roll(a, s, 0), a)
        s *= 2
    return a, b


def _hgrn_chunk(zq, zf, v, zg, lb, gn, tril, st_scr):
    chunk = zq.shape[0]
    f = lb + (1.0 - lb) * _sigmoid(zf)
    k = 1.0 - f
    q = zq * _sigmoid(zq)
    cum = _cumsum_rows(jnp.log(f))
    qd = q * jnp.exp(cum)
    kd = k * jnp.exp(-cum)
    last = cum[chunk - 1:chunk, :]
    kdec = k * jnp.exp(last - cum)
    dec = jnp.exp(last)
    outs = []
    for h in range(H_A):
        sl = slice(h * DK, (h + 1) * DK)
        att = jnp.where(tril, _bdot_nt(qd[:, sl], kd[:, sl]), 0.0)
        st = st_scr[h]
        o = _bdot(att, v[:, sl]) + _bdot_nt(qd[:, sl], st)
        st_scr[h] = st * dec[:, sl] + _bdot_tn(v[:, sl], kdec[:, sl])
        zgh = zg[:, sl]
        outs.append(_rms(o, gn[:, sl]) * (zgh * _sigmoid(zgh)))
    return outs


def _mixer_kernel(h_ref, gm_ref, win_ref, lbw_ref, gn_ref, cw_ref, cb_ref, wa_ref, ba_ref, wx_ref,
                  bx_ref, lam_ref, wout_ref, gf_ref, wq_ref, s0_ref, buf0_ref, r0_ref,
                  h1_ref, xn_ref, q_ref, s_out_ref, rlast_ref, bufnew_ref,
                  z_scr, o_scr, st_scr, xp_scr, hc_scr, *, tb, chunk, layer):
    t = pl.program_id(1)
    hist = CONV_W - 1

    @pl.when(t == 0)
    def _():
        for h in range(H_A):
            st_scr[h] = s0_ref[0, h].T
        xp_scr[CONV_PAD - hist:CONV_PAD, :] = buf0_ref[0]
        hc_scr[...] = r0_ref[0]

    hin = h_ref[0]
    z_scr[...] = _bdot(_rms(hin, gm_ref[...]), win_ref[...])

    lbw = lbw_ref[...]
    ex = jnp.exp(lbw - jnp.max(lbw, axis=0, keepdims=True))
    sm = ex / jnp.sum(ex, axis=0, keepdims=True)
    lb = jnp.sum(sm[0:layer + 1, :], axis=0, keepdims=True)
    gn = gn_ref[...]
    tril = (lax.broadcasted_iota(jnp.int32, (chunk, chunk), 0)
            >= lax.broadcasted_iota(jnp.int32, (chunk, chunk), 1))
    for c in range(tb // chunk):
        rows = pl.ds(c * chunk, chunk)
        zs = [z_scr[rows, j * W_A:(j + 1) * W_A] for j in range(4)]
        outs = _hgrn_chunk(zs[0], zs[1], zs[2], zs[3], lb, gn, tril, st_scr)
        for h in range(H_A):
            o_scr[rows, h * DK:(h + 1) * DK] = outs[h]

    x = z_scr[:, 4 * W_A:4 * W_A + W_B]
    xp_scr[CONV_PAD:CONV_PAD + tb, :] = x
    cw = cw_ref[...]
    acc = xp_scr[CONV_PAD - hist:CONV_PAD - hist + tb, :] * cw[0:1, :]
    for j in range(1, CONV_W):
        acc = acc + xp_scr[CONV_PAD - hist + j:CONV_PAD - hist + j + tb, :] * cw[j:j + 1, :]
    xc = cb_ref[...] + acc
    tail = xp_scr[CONV_PAD + tb - hist:CONV_PAD + tb, :]
    xp_scr[CONV_PAD - hist:CONV_PAD, :] = tail

    r = _sigmoid(_bdot(xc, wa_ref[...]) + ba_ref[...])
    gi = _sigmoid(_bdot(xc, wx_ref[...]) + bx_ref[...])
    nl = -lam_ref[...]
    softplus = jnp.maximum(nl, 0.0) + jnp.log1p(jnp.exp(-jnp.abs(nl)))
    log_a = -RG_C * r * softplus
    a = jnp.exp(log_a)
    one_minus_a2 = -jnp.tanh(log_a) * (a * a + 1.0)
    b = jnp.sqrt(one_minus_a2) * (gi * xc)
    aa, bb = _linear_scan_rows(a, b)
    hseq = bb + aa * hc_scr[...]
    hc_scr[...] = hseq[tb - 1:tb, :]
    o_scr[:, W_A:W_A + W_B] = hseq * _gelu_tanh(z_scr[:, 4 * W_A + W_B:4 * W_A + 2 * W_B])

    h1 = hin + _bdot(o_scr[...], wout_ref[...])
    xn = _rms(h1, gf_ref[...])
    for s in range(ROW_TILES):
        h1_ref[0, :, s, :] = h1[:, s * LANES:(s + 1) * LANES]
        xn_ref[0, :, s, :] = xn[:, s * LANES:(s + 1) * LANES]
    q_ref[0] = _bdot(xn, wq_ref[...])

    @pl.when(t == pl.num_programs(1) - 1)
    def _():
        for h in range(H_A):
            s_out_ref[0, h] = st_scr[h].T
        rlast_ref[0] = hseq[tb - 1:tb, :]
        bufnew_ref[0] = tail


def _block_diag(w):
    n, c, d = w.shape
    eye = jnp.eye(n, dtype=w.dtype)
    return (eye[:, None, :, None] * w[:, :, None, :]).reshape(n * c, n * d)


def _mixer(h, lw, lower_bounds, s0, buf0, r0, layer):
    B, L, _ = h.shape
    chunk = min(CHUNK, L)
    tb = chunk * _block(L // chunk, 4)
    nq = PEER_HEADS * PEER_DQ
    row = lambda v, w: v.reshape(1, w)
    full = lambda shape: pl.BlockSpec(shape, lambda b, t: (0,) * len(shape))
    per_batch = lambda shape: pl.BlockSpec((1,) + shape, lambda b, t: (b,) + (0,) * len(shape))
    tok3 = pl.BlockSpec((1, tb, ROW_TILES, LANES), lambda b, t: (b, t, 0, 0))
    kern = functools.partial(_mixer_kernel, tb=tb, chunk=chunk, layer=layer)
    h1, xn, q, s_new, r_new, c_new = pl.pallas_call(
        kern,
        out_shape=(jax.ShapeDtypeStruct((B, L, ROW_TILES, LANES), F32),
                   jax.ShapeDtypeStruct((B, L, ROW_TILES, LANES), F32),
                   jax.ShapeDtypeStruct((B, L, nq), F32),
                   jax.ShapeDtypeStruct((B, H_A, DK, DV), F32),
                   jax.ShapeDtypeStruct((B, 1, W_B), F32),
                   jax.ShapeDtypeStruct((B, CONV_W - 1, W_B), F32)),
        grid=(B, L // tb),
        in_specs=[pl.BlockSpec((1, tb, D_MODEL), lambda b, t: (b, t, 0)),
                  full((1, D_MODEL)), full((D_MODEL, IN_COLS)),
                  full(lower_bounds.shape), full((1, W_A)),
                  full((CONV_W, W_B)), full((1, W_B)),
                  full((W_B, W_B)), full((1, W_B)), full((W_B, W_B)), full((1, W_B)), full((1, W_B)),
                  full((W_A + W_B, D_MODEL)), full((1, D_MODEL)), full((D_MODEL, nq)),
                  per_batch((H_A, DK, DV)), per_batch((CONV_W - 1, W_B)), per_batch((1, W_B))],
        out_specs=(tok3, tok3,
                   pl.BlockSpec((1, tb, nq), lambda b, t: (b, t, 0)),
                   per_batch((H_A, DK, DV)), per_batch((1, W_B)), per_batch((CONV_W - 1, W_B))),
        scratch_shapes=[pltpu.VMEM((tb, IN_COLS), F32), pltpu.VMEM((tb, W_A + W_B), F32),
                        pltpu.VMEM((H_A, DV, DK), F32),
                        pltpu.VMEM((CONV_PAD + tb, W_B), F32), pltpu.VMEM((1, W_B), F32)],
        compiler_params=_params(
            ("arbitrary", "arbitrary"),
            _f32_bytes((tb, D_MODEL)) * 3 + _f32_bytes((tb, nq))
            + 2 * (D_MODEL * IN_COLS + 2 * W_B * W_B + (W_A + W_B) * D_MODEL + D_MODEL * nq),
            _f32_bytes((tb, IN_COLS), (tb, W_A + W_B), (H_A, DV, DK), (CONV_PAD + tb, W_B))),
        name="mixer",
    )(h, row(lw["g_mix"], D_MODEL), lw["w_in"].astype(BF16), lower_bounds,
      row(lw["g_hgrn_norm"], W_A), lw["conv_w"], row(lw["conv_b"], W_B),
      _block_diag(lw["w_rg_a"]).astype(BF16), row(lw["b_rg_a"], W_B),
      _block_diag(lw["w_rg_x"]).astype(BF16), row(lw["b_rg_x"], W_B), row(lw["lambda_rg"], W_B),
      lw["w_out"].astype(BF16), row(lw["g_ffn"], D_MODEL), lw["w_peer_q"].astype(BF16),
      s0, buf0, r0.reshape(B, 1, W_B))
    T = B * L
    return (h1.reshape(T, ROW_TILES, LANES), xn.reshape(T, ROW_TILES, LANES), q.reshape(T, nq),
            s_new, r_new.reshape(B, W_B), c_new)


def _topk_rows(vals, k, payload=None):
    n, w = vals.shape
    rows = lax.broadcasted_iota(jnp.int32, (n, w), 0).astype(F32)
    slot = lax.broadcasted_iota(jnp.int32, (k, w), 0)
    out_v = jnp.zeros((k, w), F32)
    out_p = jnp.zeros((k, w), F32)
    for i in range(k):
        m = jnp.max(vals, axis=0, keepdims=True)
        hit = jnp.min(jnp.where(vals == m, rows, float(n)), axis=0, keepdims=True)
        sel = rows == hit
        if payload is None:
            pick = hit
        else:
            pick = jnp.max(jnp.where(sel, payload, -1.0), axis=0, keepdims=True)
        out_v = jnp.where(slot == i, m, out_v)
        out_p = jnp.where(slot == i, pick, out_p)
        vals = jnp.where(sel, -jnp.inf, vals)
    return out_v, out_p


def _candidates(sv1, si1, sv2, si2):
    k, w = sv1.shape
    assert k == PEER_TOPK == 2 * SUBLANES
    row8 = lax.broadcasted_iota(jnp.int32, (SUBLANES, w), 0)
    vals = [sv1[0:1, :] + sv2, sv1[1:2, :] + sv2[0:SUBLANES, :]]
    idxs = [si1[0:1, :] * float(N_SUBKEYS) + si2, si1[1:2, :] * float(N_SUBKEYS) + si2[0:SUBLANES, :]]
    for i in range(2, SUBLANES):
        live = row8 < (k // (i + 1))
        vals.append(jnp.where(live, sv1[i:i + 1, :] + sv2[0:SUBLANES, :], -jnp.inf))
        idxs.append(si1[i:i + 1, :] * float(N_SUBKEYS) + si2[0:SUBLANES, :])
    vals.append(sv1[SUBLANES:k, :] + sv2[0:1, :])
    idxs.append(si1[SUBLANES:k, :] * float(N_SUBKEYS) + si2[0:1, :])
    return jnp.concatenate(vals, axis=0), jnp.concatenate(idxs, axis=0)


def _route_kernel(q_ref, keys_ref, idx_ref, gate_ref):
    def head(h, carry):
        col = pl.multiple_of(h * PEER_DQ, PEER_DQ)
        svs, sis = [], []
        for p in range(2):
            qs = q_ref[:, pl.ds(col + p * PEER_DHALF, PEER_DHALF)]
            s = _bdot_nt(keys_ref[h, p], qs)
            sv, si = _topk_rows(s, PEER_TOPK)
            svs.append(sv)
            sis.append(si)
        cand, cidx = _candidates(svs[0], sis[0], svs[1], sis[1])
        cs, ce = _topk_rows(cand, PEER_TOPK, payload=cidx)
        ex = jnp.exp(cs - cs[0:1, :])
        g = ex / jnp.sum(ex, axis=0, keepdims=True)
        row = pl.multiple_of(h * PEER_TOPK, PEER_TOPK)
        idx_ref[pl.ds(row, PEER_TOPK), :] = ce.astype(jnp.int32)
        gate_ref[pl.ds(row, PEER_TOPK), :] = g
        return carry

    lax.fori_loop(0, PEER_HEADS, head, 0)


def _route(q, keys):
    T = q.shape[0]
    tb = _block(T, 4 * LANES)
    nq = PEER_HEADS * PEER_DQ
    return pl.pallas_call(
        _route_kernel,
        out_shape=(jax.ShapeDtypeStruct((PEER_SEL, T), jnp.int32),
                   jax.ShapeDtypeStruct((PEER_SEL, T), F32)),
        grid=(T // tb,),
        in_specs=[pl.BlockSpec((tb, nq), lambda i: (i, 0)),
                  pl.BlockSpec(keys.shape, lambda i: (0, 0, 0, 0))],
        out_specs=(pl.BlockSpec((PEER_SEL, tb), lambda i: (0, i)),
                   pl.BlockSpec((PEER_SEL, tb), lambda i: (0, i))),
        compiler_params=_params(
            ("arbitrary",), _f32_bytes((tb, nq), (2 * PEER_SEL, tb)) + 2 * math.prod(keys.shape)),
        name="route",
    )(q, keys.astype(BF16))


GATHER_TOKENS = 64
GATHER_SLOTS = 8
DMA_PRIORITIES = 2


ACC_CHAINS = 2
ISSUE_LAG = 6
IDS_LEAD = 4


def _sum_sublanes_of_each(ps):
    assert len(ps) == SUBLANES == 8
    sub = lax.broadcasted_iota(jnp.int32, ps[0].shape, 0)
    low = sub < 4
    r = []
    for j in range(4):
        a, b = ps[j], ps[j + 4]
        r.append(jnp.where(low, a, b) + pltpu.roll(jnp.where(low, b, a), 4, 0))
    even2 = (sub & 2) == 0
    z = [jnp.where(even2, r[j] + pltpu.roll(r[j], 6, 0), r[j + 2] + pltpu.roll(r[j + 2], 2, 0))
         for j in range(2)]
    even1 = (sub & 1) == 0
    return jnp.where(even1, z[0] + pltpu.roll(z[0], 7, 0), z[1] + pltpu.roll(z[1], 1, 0))


def _gather_kernel(ids_hbm, xn_ref, h1_ref, gate_ref, uv_hbm, out_ref, uvbuf, ids_smem, sem, ids_sem,
                   *, tbg, steps_per_gate_block, n_tokens):
    step = pl.program_id(0)
    n_steps = pl.num_programs(0)
    lane0 = (step % steps_per_gate_block) * tbg
    g0 = step * tbg
    ahead = GATHER_SLOTS - 1
    ring = GATHER_SLOTS - 1

    def ids_copy(target, buf):
        row = jnp.minimum(target, n_tokens - 1)
        return pltpu.make_async_copy(ids_hbm.at[row], ids_smem.at[buf], ids_sem.at[buf])

    def issue(buf, k_lo=0, k_hi=PEER_SEL):
        for k in range(k_lo, k_hi):
            pltpu.make_async_copy(uv_hbm.at[ids_smem[buf, k]], uvbuf.at[buf, k],
                                  sem.at[buf]).start(priority=k % DMA_PRIORITIES)

    def wait_rows(slot):
        pltpu.make_async_copy(uvbuf.at[slot], uvbuf.at[slot], sem.at[slot]).wait()

    lanes = lax.broadcasted_iota(jnp.int32, (SUBLANES, gate_ref.shape[1]), 1)

    def token(tok, s):
        nxt = (s + ahead) & ring
        ids_copy(0, nxt).wait()
        ids_copy(g0 + tok + ahead + IDS_LEAD, (nxt + IDS_LEAD) & ring).start()
        wait_rows(s)
        x8 = xn_ref[tok]
        lane = lane0 + tok
        accs = [None] * ACC_CHAINS
        n_grp = PEER_SEL // SUBLANES
        for grp in range(n_grp):
            k0 = grp * SUBLANES
            prods = [uvbuf[s, k0 + j].astype(F32)[0:ROW_TILES, :] * x8 for j in range(SUBLANES)]
            g8 = jnp.sum(jnp.where(lanes == lane, gate_ref[k0:k0 + SUBLANES, :], 0.0),
                         axis=1, keepdims=True)
            chunks = [grp - ISSUE_LAG] if grp < n_grp - 1 else range(grp - ISSUE_LAG, n_grp)
            for c in chunks:
                if c >= 0:
                    issue(nxt, c * SUBLANES, (c + 1) * SUBLANES)
            act = jnp.sum(_sum_sublanes_of_each(prods), axis=1, keepdims=True)
            w = jnp.broadcast_to(g8 * _gelu_tanh(act), (SUBLANES, LANES))
            for j in range(SUBLANES):
                term = (jnp.broadcast_to(w[j:j + 1, :], (ROW_TILES, LANES))
                        * uvbuf[s, k0 + j].astype(F32)[ROW_TILES:2 * ROW_TILES, :])
                c = (k0 + j) % ACC_CHAINS
                accs[c] = term if accs[c] is None else accs[c] + term
        out_ref[tok] = h1_ref[tok] + functools.reduce(lambda a, b: a + b, accs)

    @pl.when(step == 0)
    def _():
        for t in range(GATHER_SLOTS):
            ids_copy(t, t).start()
        for t in range(ahead):
            ids_copy(0, t).wait()
            issue(t)
        for t in range(GATHER_SLOTS, ahead + IDS_LEAD):
            ids_copy(t, t & ring).start()

    def body(i, carry):
        for s in range(GATHER_SLOTS):
            token(i * GATHER_SLOTS + s, s)
        return carry

    lax.fori_loop(0, tbg // GATHER_SLOTS, body, 0)

    @pl.when(step == n_steps - 1)
    def _():
        for t in range(ahead):
            wait_rows(t)
        for t in range(IDS_LEAD):
            ids_copy(0, (ahead + t) & ring).wait()


def _gather(eidx, gates, xn, h1, peer_u, peer_v):
    T = xn.shape[0]
    tbg = _block(T, GATHER_TOKENS)
    assert tbg % GATHER_SLOTS == 0 and GATHER_SLOTS & (GATHER_SLOTS - 1) == 0
    assert 0 < IDS_LEAD < GATHER_SLOTS
    gate_tb = _block(T, LANES)
    per = gate_tb // tbg
    n_exp = peer_u.shape[0]
    uv = jnp.concatenate([peer_u.reshape(n_exp, ROW_TILES, LANES),
                          peer_v.reshape(n_exp, ROW_TILES, LANES)], axis=1).astype(BF16)
    kern = functools.partial(_gather_kernel, tbg=tbg, steps_per_gate_block=per, n_tokens=T)
    tok_spec = pl.BlockSpec((tbg, ROW_TILES, LANES), lambda i: (i, 0, 0))
    return pl.pallas_call(
        kern,
        out_shape=jax.ShapeDtypeStruct((T, ROW_TILES, LANES), F32),
        grid=(T // tbg,),
        in_specs=[pl.BlockSpec(memory_space=pl.ANY),
                  tok_spec, tok_spec,
                  pl.BlockSpec((PEER_SEL, gate_tb), lambda i: (0, i // per)),
                  pl.BlockSpec(memory_space=pl.ANY)],
        out_specs=tok_spec,
        scratch_shapes=[pltpu.VMEM((GATHER_SLOTS, PEER_SEL, 2 * ROW_TILES, LANES), BF16),
                        pltpu.SMEM((GATHER_SLOTS, PEER_SEL), jnp.int32),
                        pltpu.SemaphoreType.DMA((GATHER_SLOTS,)),
                        pltpu.SemaphoreType.DMA((GATHER_SLOTS,))],
        compiler_params=_params(
            ("arbitrary",), _f32_bytes((tbg, D_MODEL)) * 3 + _f32_bytes((PEER_SEL, gate_tb)),
            _f32_bytes((GATHER_SLOTS, PEER_SEL, D_MODEL))),
        name="gather",
    )(eidx.T, xn, h1, gates, uv)


def _ple_kernel(h_ref, p_ref, wp_ref, gp_ref, wg_ref, gf_ref, y_ref):
    h2 = jnp.concatenate([h_ref[:, s, :] for s in range(ROW_TILES)], axis=1)
    gate = _sigmoid(_bdot(_rms(h2, gp_ref[...]), wg_ref[...]))
    h3 = h2 + _bdot(p_ref[...], wp_ref[...]) * gate
    y_ref[...] = _rms(h3, gf_ref[...])


def _ple(h2, p2d, w_ple, g_ple, w_gate, g_final):
    T = h2.shape[0]
    tb = _block(T, 256)
    tok = lambda w: pl.BlockSpec((tb, w), lambda i: (i, 0))
    full = lambda shape: pl.BlockSpec(shape, lambda i: (0, 0))
    return pl.pallas_call(
        _ple_kernel,
        out_shape=jax.ShapeDtypeStruct((T, D_MODEL), F32),
        grid=(T // tb,),
        in_specs=[pl.BlockSpec((tb, ROW_TILES, LANES), lambda i: (i, 0, 0)),
                  tok(PLE_DIM), full((PLE_DIM, D_MODEL)), full((1, D_MODEL)),
                  full((D_MODEL, D_MODEL)), full((1, D_MODEL))],
        out_specs=tok(D_MODEL),
        compiler_params=_params(
            ("arbitrary",), _f32_bytes((tb, D_MODEL)) * 2 + _f32_bytes((tb, PLE_DIM))
            + 2 * (PLE_DIM * D_MODEL + D_MODEL * D_MODEL)),
        name="ple",
    )(h2, p2d, w_ple.astype(BF16), g_ple.reshape(1, D_MODEL), w_gate.astype(BF16),
      g_final.reshape(1, D_MODEL))


def _run_stream(h, p, s0, r0, c0, lw, g_final, lower_bounds, layer):
    B, L, _ = h.shape
    T = B * L
    h1, xn, q, s_new, r_new, c_new = _mixer(h, lw, lower_bounds, s0, c0, r0, layer)
    eidx, gates = _route(q, lw["peer_keys"])
    h2 = _gather(eidx, gates, xn, h1, lw["peer_u"], lw["peer_v"])
    y = _ple(h2, p.reshape(T, PLE_DIM), lw["w_ple"], lw["g_ple"], lw["w_ple_gate"], g_final)
    return y.reshape(B, L, D_MODEL), s_new, r_new, c_new


def kernel(x_prompt, x_sample, state_hgrn, state_rglru, state_conv, p_prompt, p_sample, g_mix, w_in, lower_bounds, g_hgrn_norm, conv_w, conv_b, w_rg_a, b_rg_a, w_rg_x, b_rg_x, lambda_rg, w_out, g_ffn, w_peer_q, peer_keys, peer_u, peer_v, g_ple, w_ple_gate, w_ple, g_final):
    depth = w_in.shape[0]
    assert depth == 1, "the final norm is fused into the last layer's kernel; depth 1 only"
    layer = 0
    lw = dict(g_mix=g_mix[layer], w_in=w_in[layer], g_hgrn_norm=g_hgrn_norm[layer],
              conv_w=conv_w[layer], conv_b=conv_b[layer], w_rg_a=w_rg_a[layer], b_rg_a=b_rg_a[layer],
              w_rg_x=w_rg_x[layer], b_rg_x=b_rg_x[layer], lambda_rg=lambda_rg[layer],
              w_out=w_out[layer], g_ffn=g_ffn[layer], w_peer_q=w_peer_q[layer],
              peer_keys=peer_keys[layer], peer_u=peer_u[layer], peer_v=peer_v[layer],
              g_ple=g_ple[layer], w_ple_gate=w_ple_gate[layer], w_ple=w_ple[layer])
    bp = x_prompt.shape[0]
    y_p, s_p, r_p, c_p = _run_stream(
        x_prompt, p_prompt[layer], jnp.zeros((bp, H_A, DK, DV), F32), jnp.zeros((bp, W_B), F32),
        jnp.zeros((bp, CONV_W - 1, W_B), F32), lw, g_final, lower_bounds, layer)
    y_s, s_s, r_s, c_s = _run_stream(
        x_sample, p_sample[layer], state_hgrn[layer], state_rglru[layer], state_conv[layer],
        lw, g_final, lower_bounds, layer)
    return (y_p, y_s, s_p[None], r_p[None], c_p[None], s_s[None], r_s[None], c_s[None])
```

```python
import functools
import math

import jax
import jax.numpy as jnp
from jax import lax
from jax.experimental import pallas as pl
from jax.experimental.pallas import tpu as pltpu
from jax.experimental.pallas import tpu_sc as plsc

D_MODEL = 1024
W_A = 512
DK = 128
DV = 128
H_A = W_A // DK
W_B = 512
CONV_W = 4
RG_C = 8.0
IN_COLS = 4 * W_A + 2 * W_B
CHUNK = 64
PEER_HEADS = 8
N_SUBKEYS = 128
PEER_TOPK = 16
PEER_DQ = 256
PEER_DHALF = PEER_DQ // 2
PEER_SEL = PEER_HEADS * PEER_TOPK
PLE_DIM = 256
EPS = 1e-6

LANES = 128
SUBLANES = 8
CONV_PAD = SUBLANES
ROW_TILES = D_MODEL // LANES
MIB = 1024 * 1024
VMEM_BYTES_V7X = 64 * MIB
VMEM_MIN_REQUEST = 16 * MIB

F32 = jnp.float32
BF16 = jnp.bfloat16


def _block(n, target):
    b = min(n, target)
    while n % b:
        b -= 1
    return b


def _params(semantics, block_bytes, scratch_bytes=0):
    need = 3 * block_bytes + scratch_bytes
    limit = min(max(need, VMEM_MIN_REQUEST), VMEM_BYTES_V7X // 8 * 7)
    return pltpu.CompilerParams(dimension_semantics=semantics, vmem_limit_bytes=limit)


def _f32_bytes(*shapes):
    return sum(4 * math.prod(s) for s in shapes)


def _rms(x, g):
    ms = jnp.mean(x * x, axis=-1, keepdims=True)
    return x * lax.rsqrt(ms + EPS) * g


def _sigmoid(x):
    return 1.0 / (1.0 + jnp.exp(-x))


def _gelu_tanh(x):
    c = math.sqrt(2.0 / math.pi)
    return x * (0.5 * (1.0 + jnp.tanh(c * (x + 0.044715 * (x * x * x)))))


def _bdot(a, b):
    return jnp.dot(a.astype(BF16), b.astype(BF16), preferred_element_type=F32)


def _bdot_nt(a, b):
    return lax.dot_general(a.astype(BF16), b.astype(BF16), (((1,), (1,)), ((), ())),
                           preferred_element_type=F32)


def _bdot_tn(a, b):
    return lax.dot_general(a.astype(BF16), b.astype(BF16), (((0,), (0,)), ((), ())),
                           preferred_element_type=F32)


def _cumsum_rows(x):
    n = x.shape[0]
    rows = lax.broadcasted_iota(jnp.int32, x.shape, 0)
    s = 1
    while s < n:
        x = x + jnp.where(rows >= s, pltpu.roll(x, s, 0), 0.0)
        s *= 2
    return x


def _linear_scan_rows(a, b):
    n = a.shape[0]
    rows = lax.broadcasted_iota(jnp.int32, a.shape, 0)
    s = 1
    while s < n:
        keep = rows >= s
        b = jnp.where(keep, a * pltpu.roll(b, s, 0) + b, b)
        a = jnp.where(keep, a * pltpu.roll(a, s, 0), a)
        s *= 2
    return a, b


def _hgrn_chunk(zq, zf, v, zg, lb, gn, tril, st_scr):
    chunk = zq.shape[0]
    f = lb + (1.0 - lb) * _sigmoid(zf)
    k = 1.0 - f
    q = zq * _sigmoid(zq)
    cum = _cumsum_rows(jnp.log(f))
    qd = q * jnp.exp(cum)
    kd = k * jnp.exp(-cum)
    last = cum[chunk - 1:chunk, :]
    kdec = k * jnp.exp(last - cum)
    dec = jnp.exp(last)
    outs = []
    for h in range(H_A):
        sl = slice(h * DK, (h + 1) * DK)
        att = jnp.where(tril, _bdot_nt(qd[:, sl], kd[:, sl]), 0.0)
        st = st_scr[h]
        o = _bdot(att, v[:, sl]) + _bdot_nt(qd[:, sl], st)
        st_scr[h] = st * dec[:, sl] + _bdot_tn(v[:, sl], kdec[:, sl])
        zgh = zg[:, sl]
        outs.append(_rms(o, gn[:, sl]) * (zgh * _sigmoid(zgh)))
    return outs


def _mixer_kernel(h_ref, gm_ref, win_ref, lbw_ref, gn_ref, cw_ref, cb_ref, wa_ref, ba_ref, wx_ref,
                  bx_ref, lam_ref, wout_ref, gf_ref, wq_ref, s0_ref, buf0_ref, r0_ref,
                  h1_ref, xn_ref, q_ref, s_out_ref, rlast_ref, bufnew_ref,
                  z_scr, o_scr, st_scr, xp_scr, hc_scr, *, tb, chunk, layer):
    t = pl.program_id(1)
    hist = CONV_W - 1

    @pl.when(t == 0)
    def _():
        for h in range(H_A):
            st_scr[h] = s0_ref[0, h].T
        xp_scr[CONV_PAD - hist:CONV_PAD, :] = buf0_ref[0]
        hc_scr[...] = r0_ref[0]

    hin = h_ref[0]
    z_scr[...] = _bdot(_rms(hin, gm_ref[...]), win_ref[...])

    lbw = lbw_ref[...]
    ex = jnp.exp(lbw - jnp.max(lbw, axis=0, keepdims=True))
    sm = ex / jnp.sum(ex, axis=0, keepdims=True)
    lb = jnp.sum(sm[0:layer + 1, :], axis=0, keepdims=True)
    gn = gn_ref[...]
    tril = (lax.broadcasted_iota(jnp.int32, (chunk, chunk), 0)
            >= lax.broadcasted_iota(jnp.int32, (chunk, chunk), 1))
    for c in range(tb // chunk):
        rows = pl.ds(c * chunk, chunk)
        zs = [z_scr[rows, j * W_A:(j + 1) * W_A] for j in range(4)]
        outs = _hgrn_chunk(zs[0], zs[1], zs[2], zs[3], lb, gn, tril, st_scr)
        for h in range(H_A):
            o_scr[rows, h * DK:(h + 1) * DK] = outs[h]

    x = z_scr[:, 4 * W_A:4 * W_A + W_B]
    xp_scr[CONV_PAD:CONV_PAD + tb, :] = x
    cw = cw_ref[...]
    acc = xp_scr[CONV_PAD - hist:CONV_PAD - hist + tb, :] * cw[0:1, :]
    for j in range(1, CONV_W):
        acc = acc + xp_scr[CONV_PAD - hist + j:CONV_PAD - hist + j + tb, :] * cw[j:j + 1, :]
    xc = cb_ref[...] + acc
    tail = xp_scr[CONV_PAD + tb - hist:CONV_PAD + tb, :]
    xp_scr[CONV_PAD - hist:CONV_PAD, :] = tail

    r = _sigmoid(_bdot(xc, wa_ref[...]) + ba_ref[...])
    gi = _sigmoid(_bdot(xc, wx_ref[...]) + bx_ref[...])
    nl = -lam_ref[...]
    softplus = jnp.maximum(nl, 0.0) + jnp.log1p(jnp.exp(-jnp.abs(nl)))
    log_a = -RG_C * r * softplus
    a = jnp.exp(log_a)
    one_minus_a2 = -jnp.tanh(log_a) * (a * a + 1.0)
    b = jnp.sqrt(one_minus_a2) * (gi * xc)
    aa, bb = _linear_scan_rows(a, b)
    hseq = bb + aa * hc_scr[...]
    hc_scr[...] = hseq[tb - 1:tb, :]
    o_scr[:, W_A:W_A + W_B] = hseq * _gelu_tanh(z_scr[:, 4 * W_A + W_B:4 * W_A + 2 * W_B])

    h1 = hin + _bdot(o_scr[...], wout_ref[...])
    xn = _rms(h1, gf_ref[...])
    for s in range(ROW_TILES):
        h1_ref[0, :, s, :] = h1[:, s * LANES:(s + 1) * LANES]
        xn_ref[0, :, s, :] = xn[:, s * LANES:(s + 1) * LANES]
    q_ref[0] = _bdot(xn, wq_ref[...])

    @pl.when(t == pl.num_programs(1) - 1)
    def _():
        for h in range(H_A):
            s_out_ref[0, h] = st_scr[h].T
        rlast_ref[0] = hseq[tb - 1:tb, :]
        bufnew_ref[0] = tail


def _block_diag(w):
    n, c, d = w.shape
    eye = jnp.eye(n, dtype=w.dtype)
    return (eye[:, None, :, None] * w[:, :, None, :]).reshape(n * c, n * d)


def _mixer(h, lw, lower_bounds, s0, buf0, r0, layer):
    B, L, _ = h.shape
    chunk = min(CHUNK, L)
    tb = chunk * _block(L // chunk, 4)
    nq = PEER_HEADS * PEER_DQ
    row = lambda v, w: v.reshape(1, w)
    full = lambda shape: pl.BlockSpec(shape, lambda b, t: (0,) * len(shape))
    per_batch = lambda shape: pl.BlockSpec((1,) + shape, lambda b, t: (b,) + (0,) * len(shape))
    tok3 = pl.BlockSpec((1, tb, ROW_TILES, LANES), lambda b, t: (b, t, 0, 0))
    kern = functools.partial(_mixer_kernel, tb=tb, chunk=chunk, layer=layer)
    h1, xn, q, s_new, r_new, c_new = pl.pallas_call(
        kern,
        out_shape=(jax.ShapeDtypeStruct((B, L, ROW_TILES, LANES), F32),
                   jax.ShapeDtypeStruct((B, L, ROW_TILES, LANES), F32),
                   jax.ShapeDtypeStruct((B, L, nq), F32),
                   jax.ShapeDtypeStruct((B, H_A, DK, DV), F32),
                   jax.ShapeDtypeStruct((B, 1, W_B), F32),
                   jax.ShapeDtypeStruct((B, CONV_W - 1, W_B), F32)),
        grid=(B, L // tb),
        in_specs=[pl.BlockSpec((1, tb, D_MODEL), lambda b, t: (b, t, 0)),
                  full((1, D_MODEL)), full((D_MODEL, IN_COLS)),
                  full(lower_bounds.shape), full((1, W_A)),
                  full((CONV_W, W_B)), full((1, W_B)),
                  full((W_B, W_B)), full((1, W_B)), full((W_B, W_B)), full((1, W_B)), full((1, W_B)),
                  full((W_A + W_B, D_MODEL)), full((1, D_MODEL)), full((D_MODEL, nq)),
                  per_batch((H_A, DK, DV)), per_batch((CONV_W - 1, W_B)), per_batch((1, W_B))],
        out_specs=(tok3, tok3,
                   pl.BlockSpec((1, tb, nq), lambda b, t: (b, t, 0)),
                   per_batch((H_A, DK, DV)), per_batch((1, W_B)), per_batch((CONV_W - 1, W_B))),
        scratch_shapes=[pltpu.VMEM((tb, IN_COLS), F32), pltpu.VMEM((tb, W_A + W_B), F32),
                        pltpu.VMEM((H_A, DV, DK), F32),
                        pltpu.VMEM((CONV_PAD + tb, W_B), F32), pltpu.VMEM((1, W_B), F32)],
        compiler_params=_params(
            ("arbitrary", "arbitrary"),
            _f32_bytes((tb, D_MODEL)) * 3 + _f32_bytes((tb, nq))
            + 2 * (D_MODEL * IN_COLS + 2 * W_B * W_B + (W_A + W_B) * D_MODEL + D_MODEL * nq),
            _f32_bytes((tb, IN_COLS), (tb, W_A + W_B), (H_A, DV, DK), (CONV_PAD + tb, W_B))),
        name="mixer",
    )(h, row(lw["g_mix"], D_MODEL), lw["w_in"].astype(BF16), lower_bounds,
      row(lw["g_hgrn_norm"], W_A), lw["conv_w"], row(lw["conv_b"], W_B),
      _block_diag(lw["w_rg_a"]).astype(BF16), row(lw["b_rg_a"], W_B),
      _block_diag(lw["w_rg_x"]).astype(BF16), row(lw["b_rg_x"], W_B), row(lw["lambda_rg"], W_B),
      lw["w_out"].astype(BF16), row(lw["g_ffn"], D_MODEL), lw["w_peer_q"].astype(BF16),
      s0, buf0, r0.reshape(B, 1, W_B))
    T = B * L
    return (h1.reshape(T, ROW_TILES, LANES), xn.reshape(T, ROW_TILES, LANES), q.reshape(T, nq),
            s_new, r_new.reshape(B, W_B), c_new)


def _topk_rows(vals, k, payload=None):
    n, w = vals.shape
    rows = lax.broadcasted_iota(jnp.int32, (n, w), 0).astype(F32)
    slot = lax.broadcasted_iota(jnp.int32, (k, w), 0)
    out_v = jnp.zeros((k, w), F32)
    out_p = jnp.zeros((k, w), F32)
    for i in range(k):
        m = jnp.max(vals, axis=0, keepdims=True)
        hit = jnp.min(jnp.where(vals == m, rows, float(n)), axis=0, keepdims=True)
        sel = rows == hit
        if payload is None:
            pick = hit
        else:
            pick = jnp.max(jnp.where(sel, payload, -1.0), axis=0, keepdims=True)
        out_v = jnp.where(slot == i, m, out_v)
        out_p = jnp.where(slot == i, pick, out_p)
        vals = jnp.where(sel, -jnp.inf, vals)
    return out_v, out_p


def _candidates(sv1, si1, sv2, si2):
    k, w = sv1.shape
    assert k == PEER_TOPK == 2 * SUBLANES
    row8 = lax.broadcasted_iota(jnp.int32, (SUBLANES, w), 0)
    vals = [sv1[0:1, :] + sv2, sv1[1:2, :] + sv2[0:SUBLANES, :]]
    idxs = [si1[0:1, :] * float(N_SUBKEYS) + si2, si1[1:2, :] * float(N_SUBKEYS) + si2[0:SUBLANES, :]]
    for i in range(2, SUBLANES):
        live = row8 < (k // (i + 1))
        vals.append(jnp.where(live, sv1[i:i + 1, :] + sv2[0:SUBLANES, :], -jnp.inf))
        idxs.append(si1[i:i + 1, :] * float(N_SUBKEYS) + si2[0:SUBLANES, :])
    vals.append(sv1[SUBLANES:k, :] + sv2[0:1, :])
    idxs.append(si1[SUBLANES:k, :] * float(N_SUBKEYS) + si2[0:1, :])
    return jnp.concatenate(vals, axis=0), jnp.concatenate(idxs, axis=0)


def _route_kernel(q_ref, keys_ref, idx_ref, gate_ref):
    def head(h, carry):
        col = pl.multiple_of(h * PEER_DQ, PEER_DQ)
        svs, sis = [], []
        for p in range(2):
            qs = q_ref[:, pl.ds(col + p * PEER_DHALF, PEER_DHALF)]
            s = _bdot_nt(keys_ref[h, p], qs)
            sv, si = _topk_rows(s, PEER_TOPK)
            svs.append(sv)
            sis.append(si)
        cand, cidx = _candidates(svs[0], sis[0], svs[1], sis[1])
        cs, ce = _topk_rows(cand, PEER_TOPK, payload=cidx)
        ex = jnp.exp(cs - cs[0:1, :])
        g = ex / jnp.sum(ex, axis=0, keepdims=True)
        row = pl.multiple_of(h * PEER_TOPK, PEER_TOPK)
        idx_ref[pl.ds(row, PEER_TOPK), :] = ce.astype(jnp.int32)
        gate_ref[pl.ds(row, PEER_TOPK), :] = g
        return carry

    lax.fori_loop(0, PEER_HEADS, head, 0)


def _route(q, keys):
    T = q.shape[0]
    tb = _block(T, 4 * LANES)
    nq = PEER_HEADS * PEER_DQ
    return pl.pallas_call(
        _route_kernel,
        out_shape=(jax.ShapeDtypeStruct((PEER_SEL, T), jnp.int32),
                   jax.ShapeDtypeStruct((PEER_SEL, T), F32)),
        grid=(T // tb,),
        in_specs=[pl.BlockSpec((tb, nq), lambda i: (i, 0)),
                  pl.BlockSpec(keys.shape, lambda i: (0, 0, 0, 0))],
        out_specs=(pl.BlockSpec((PEER_SEL, tb), lambda i: (0, i)),
                   pl.BlockSpec((PEER_SEL, tb), lambda i: (0, i))),
        compiler_params=_params(
            ("arbitrary",), _f32_bytes((tb, nq), (2 * PEER_SEL, tb)) + 2 * math.prod(keys.shape)),
        name="route",
    )(q, keys.astype(BF16))


GATHER_TOKENS = 64
GATHER_SLOTS = 8
DMA_PRIORITIES = 2


ACC_CHAINS = 2
ISSUE_LAG = 6
IDS_LEAD = 4
STAGED_NUM, STAGED_DEN = 3, 8
SC_ROWS = 64


def _sum_sublanes_of_each(ps):
    assert len(ps) == SUBLANES == 8
    sub = lax.broadcasted_iota(jnp.int32, ps[0].shape, 0)
    low = sub < 4
    r = []
    for j in range(4):
        a, b = ps[j], ps[j + 4]
        r.append(jnp.where(low, a, b) + pltpu.roll(jnp.where(low, b, a), 4, 0))
    even2 = (sub & 2) == 0
    z = [jnp.where(even2, r[j] + pltpu.roll(r[j], 6, 0), r[j + 2] + pltpu.roll(r[j + 2], 2, 0))
         for j in range(2)]
    even1 = (sub & 1) == 0
    return jnp.where(even1, z[0] + pltpu.roll(z[0], 7, 0), z[1] + pltpu.roll(z[1], 1, 0))


def _expert_uv(uvbuf, s, k):
    packed = uvbuf[s, k]
    u = lax.bitcast_convert_type(packed << 16, F32)
    v = lax.bitcast_convert_type(packed & jnp.uint32(0xFFFF0000), F32)
    return u, v


def _consume_token(uvbuf, s, x8, gate_ref, lane, lanes, issue_chunk=None):
    accs = [None] * ACC_CHAINS
    n_grp = PEER_SEL // SUBLANES
    for grp in range(n_grp):
        k0 = grp * SUBLANES
        uvs = [_expert_uv(uvbuf, s, k0 + j) for j in range(SUBLANES)]
        prods = [u * x8 for u, _ in uvs]
        g8 = jnp.sum(jnp.where(lanes == lane, gate_ref[k0:k0 + SUBLANES, :], 0.0),
                     axis=1, keepdims=True)
        if issue_chunk is not None:
            chunks = [grp - ISSUE_LAG] if grp < n_grp - 1 else range(grp - ISSUE_LAG, n_grp)
            for c in chunks:
                if c >= 0:
                    issue_chunk(c)
        act = jnp.sum(_sum_sublanes_of_each(prods), axis=1, keepdims=True)
        w = jnp.broadcast_to(g8 * _gelu_tanh(act), (SUBLANES, LANES))
        for j in range(SUBLANES):
            term = jnp.broadcast_to(w[j:j + 1, :], (ROW_TILES, LANES)) * uvs[j][1]
            c = (k0 + j) % ACC_CHAINS
            accs[c] = term if accs[c] is None else accs[c] + term
    return functools.reduce(lambda a, b: a + b, accs)


def _gather_kernel(ids_hbm, xn_ref, h1_ref, gate_ref, uv_hbm, out_ref, uvbuf, ids_smem, sem, ids_sem,
                   *, tbg, steps_per_gate_block, n_tokens):
    step = pl.program_id(0)
    n_steps = pl.num_programs(0)
    lane0 = (step % steps_per_gate_block) * tbg
    g0 = step * tbg
    ahead = GATHER_SLOTS - 1
    ring = GATHER_SLOTS - 1

    def ids_copy(target, buf):
        row = jnp.minimum(target, n_tokens - 1)
        return pltpu.make_async_copy(ids_hbm.at[row], ids_smem.at[buf], ids_sem.at[buf])

    def issue(buf, k_lo=0, k_hi=PEER_SEL):
        for k in range(k_lo, k_hi):
            pltpu.make_async_copy(uv_hbm.at[ids_smem[buf, k]], uvbuf.at[buf, k],
                                  sem.at[buf]).start(priority=k % DMA_PRIORITIES)

    def wait_rows(slot):
        pltpu.make_async_copy(uvbuf.at[slot], uvbuf.at[slot], sem.at[slot]).wait()

    lanes = lax.broadcasted_iota(jnp.int32, (SUBLANES, gate_ref.shape[1]), 1)

    def token(tok, s):
        nxt = (s + ahead) & ring
        ids_copy(0, nxt).wait()
        ids_copy(g0 + tok + ahead + IDS_LEAD, (nxt + IDS_LEAD) & ring).start()
        wait_rows(s)
        peer = _consume_token(uvbuf, s, xn_ref[tok], gate_ref, lane0 + tok, lanes,
                              lambda c: issue(nxt, c * SUBLANES, (c + 1) * SUBLANES))
        out_ref[tok] = h1_ref[tok] + peer

    @pl.when(step == 0)
    def _():
        for t in range(GATHER_SLOTS):
            ids_copy(t, t).start()
        for t in range(ahead):
            ids_copy(0, t).wait()
            issue(t)
        for t in range(GATHER_SLOTS, ahead + IDS_LEAD):
            ids_copy(t, t & ring).start()

    def body(i, carry):
        for s in range(GATHER_SLOTS):
            token(i * GATHER_SLOTS + s, s)
        return carry

    lax.fori_loop(0, tbg // GATHER_SLOTS, body, 0)

    @pl.when(step == n_steps - 1)
    def _():
        for t in range(ahead):
            wait_rows(t)
        for t in range(IDS_LEAD):
            ids_copy(0, (ahead + t) & ring).wait()


def _gather_staged_kernel(staged_hbm, xn_ref, h1_ref, gate_ref, out_alias_hbm, out_ref, uvbuf, sem,
                          *, tbg, steps_per_gate_block, step0, n_staged):
    del out_alias_hbm
    step = pl.program_id(0)
    n_steps = pl.num_programs(0)
    lane0 = ((step0 + step) % steps_per_gate_block) * tbg
    g0 = step * tbg
    ahead = GATHER_SLOTS - 1
    ring = GATHER_SLOTS - 1

    def rows_copy(target, slot):
        row = jnp.minimum(target, n_staged - 1)
        return pltpu.make_async_copy(staged_hbm.at[row], uvbuf.at[slot], sem.at[slot])

    lanes = lax.broadcasted_iota(jnp.int32, (SUBLANES, gate_ref.shape[1]), 1)

    @pl.when(step == 0)
    def _():
        for t in range(ahead):
            rows_copy(t, t).start()

    def body(tok, carry):
        slot = tok & ring
        rows_copy(g0 + tok + ahead, (tok + ahead) & ring).start()
        rows_copy(0, slot).wait()
        peer = _consume_token(uvbuf, slot, xn_ref[tok], gate_ref, lane0 + tok, lanes)
        out_ref[tok] = h1_ref[tok] + peer
        return carry

    lax.fori_loop(0, tbg, body, 0)

    @pl.when(step == n_steps - 1)
    def _():
        for t in range(ahead):
            rows_copy(0, t).wait()


def _sc_stage(uv, ids_flat):
    info = plsc.get_sparse_core_info()
    n_cores, n_sub = info.num_cores, info.num_subcores
    n = ids_flat.shape[0]
    per_worker = n // (n_cores * n_sub)
    assert per_worker * n_cores * n_sub == n and per_worker % SC_ROWS == 0
    mesh = plsc.VectorSubcoreMesh(core_axis_name="c", subcore_axis_name="s")

    @functools.partial(
        pl.kernel, mesh=mesh,
        out_type=jax.ShapeDtypeStruct((n, ROW_TILES, LANES), jnp.uint32),
        scratch_types=[pltpu.VMEM((SC_ROWS,), jnp.int32),
                       pltpu.VMEM((SC_ROWS, ROW_TILES, LANES), jnp.uint32),
                       pltpu.SemaphoreType.DMA])
    def stage(uv_hbm, ids_hbm, out_hbm, idx_v, rows_v, sem):
        worker = lax.axis_index("s") * n_cores + lax.axis_index("c")
        base = worker * per_worker

        @pl.loop(0, per_worker // SC_ROWS)
        def _(i):
            off = pl.multiple_of(base + i * SC_ROWS, SC_ROWS)
            pltpu.sync_copy(ids_hbm.at[pl.ds(off, SC_ROWS)], idx_v)
            pltpu.async_copy(uv_hbm.at[idx_v], rows_v, sem).wait()
            pltpu.sync_copy(rows_v, out_hbm.at[pl.ds(off, SC_ROWS)])

    return stage(uv, ids_flat)


def _gather(eidx, gates, xn, h1, peer_u, peer_v):
    T = xn.shape[0]
    tbg = _block(T, GATHER_TOKENS)
    assert tbg % GATHER_SLOTS == 0 and GATHER_SLOTS & (GATHER_SLOTS - 1) == 0
    assert 0 < IDS_LEAD < GATHER_SLOTS
    gate_tb = _block(T, LANES)
    per = gate_tb // tbg
    n_exp = peer_u.shape[0]
    t_staged = (T * STAGED_NUM // STAGED_DEN) // gate_tb * gate_tb
    t_direct = T - t_staged
    halves = [lax.bitcast_convert_type(w.reshape(n_exp, ROW_TILES, LANES).astype(BF16), jnp.uint16)
              .astype(jnp.uint32) for w in (peer_u, peer_v)]
    uv = halves[0] | (halves[1] << 16)
    ids = eidx.T
    tok_spec = pl.BlockSpec((tbg, ROW_TILES, LANES), lambda i: (i, 0, 0))
    block_bytes = _f32_bytes((tbg, D_MODEL)) * 3 + _f32_bytes((PEER_SEL, gate_tb))
    slot_bytes = _f32_bytes((GATHER_SLOTS, PEER_SEL, D_MODEL))
    out = pl.pallas_call(
        functools.partial(_gather_kernel, tbg=tbg, steps_per_gate_block=per, n_tokens=t_direct),
        out_shape=jax.ShapeDtypeStruct((T, ROW_TILES, LANES), F32),
        grid=(t_direct // tbg,),
        in_specs=[pl.BlockSpec(memory_space=pl.ANY),
                  tok_spec, tok_spec,
                  pl.BlockSpec((PEER_SEL, gate_tb), lambda i: (0, i // per)),
                  pl.BlockSpec(memory_space=pl.ANY)],
        out_specs=tok_spec,
        scratch_shapes=[pltpu.VMEM((GATHER_SLOTS, PEER_SEL, ROW_TILES, LANES), jnp.uint32),
                        pltpu.SMEM((GATHER_SLOTS, PEER_SEL), jnp.int32),
                        pltpu.SemaphoreType.DMA((GATHER_SLOTS,)),
                        pltpu.SemaphoreType.DMA((GATHER_SLOTS,))],
        compiler_params=_params(("arbitrary",), block_bytes, slot_bytes),
        name="gather",
    )(ids, xn, h1, gates, uv)
    if t_staged == 0:
        return out
    staged = _sc_stage(uv, ids[t_direct:].reshape(t_staged * PEER_SEL))
    step0 = t_direct // tbg
    off_spec = pl.BlockSpec((tbg, ROW_TILES, LANES), lambda i: (step0 + i, 0, 0))
    return pl.pallas_call(
        functools.partial(_gather_staged_kernel, tbg=tbg, steps_per_gate_block=per, step0=step0,
                          n_staged=t_staged),
        out_shape=jax.ShapeDtypeStruct((T, ROW_TILES, LANES), F32),
        grid=(t_staged // tbg,),
        in_specs=[pl.BlockSpec(memory_space=pl.ANY),
                  off_spec, off_spec,
                  pl.BlockSpec((PEER_SEL, gate_tb), lambda i: (0, (step0 + i) // per)),
                  pl.BlockSpec(memory_space=pl.ANY)],
        out_specs=off_spec,
        scratch_shapes=[pltpu.VMEM((GATHER_SLOTS, PEER_SEL, ROW_TILES, LANES), jnp.uint32),
                        pltpu.SemaphoreType.DMA((GATHER_SLOTS,))],
        input_output_aliases={4: 0},
        compiler_params=_params(("arbitrary",), block_bytes, slot_bytes),
        name="gather_staged",
    )(staged.reshape(t_staged, PEER_SEL, ROW_TILES, LANES), xn, h1, gates, out)


def _ple_kernel(h_ref, p_ref, wp_ref, gp_ref, wg_ref, gf_ref, y_ref):
    h2 = jnp.concatenate([h_ref[:, s, :] for s in range(ROW_TILES)], axis=1)
    gate = _sigmoid(_bdot(_rms(h2, gp_ref[...]), wg_ref[...]))
    h3 = h2 + _bdot(p_ref[...], wp_ref[...]) * gate
    y_ref[...] = _rms(h3, gf_ref[...])


def _ple(h2, p2d, w_ple, g_ple, w_gate, g_final):
    T = h2.shape[0]
    tb = _block(T, 256)
    tok = lambda w: pl.BlockSpec((tb, w), lambda i: (i, 0))
    full = lambda shape: pl.BlockSpec(shape, lambda i: (0, 0))
    return pl.pallas_call(
        _ple_kernel,
        out_shape=jax.ShapeDtypeStruct((T, D_MODEL), F32),
        grid=(T // tb,),
        in_specs=[pl.BlockSpec((tb, ROW_TILES, LANES), lambda i: (i, 0, 0)),
                  tok(PLE_DIM), full((PLE_DIM, D_MODEL)), full((1, D_MODEL)),
                  full((D_MODEL, D_MODEL)), full((1, D_MODEL))],
        out_specs=tok(D_MODEL),
        compiler_params=_params(
            ("arbitrary",), _f32_bytes((tb, D_MODEL)) * 2 + _f32_bytes((tb, PLE_DIM))
            + 2 * (PLE_DIM * D_MODEL + D_MODEL * D_MODEL)),
        name="ple",
    )(h2, p2d, w_ple.astype(BF16), g_ple.reshape(1, D_MODEL), w_gate.astype(BF16),
      g_final.reshape(1, D_MODEL))


def _run_stream(h, p, s0, r0, c0, lw, g_final, lower_bounds, layer):
    B, L, _ = h.shape
    T = B * L
    h1, xn, q, s_new, r_new, c_new = _mixer(h, lw, lower_bounds, s0, c0, r0, layer)
    eidx, gates = _route(q, lw["peer_keys"])
    h2 = _gather(eidx, gates, xn, h1, lw["peer_u"], lw["peer_v"])
    y = _ple(h2, p.reshape(T, PLE_DIM), lw["w_ple"], lw["g_ple"], lw["w_ple_gate"], g_final)
    return y.reshape(B, L, D_MODEL), s_new, r_new, c_new


def kernel(x_prompt, x_sample, state_hgrn, state_rglru, state_conv, p_prompt, p_sample, g_mix, w_in, lower_bounds, g_hgrn_norm, conv_w, conv_b, w_rg_a, b_rg_a, w_rg_x, b_rg_x, lambda_rg, w_out, g_ffn, w_peer_q, peer_keys, peer_u, peer_v, g_ple, w_ple_gate, w_ple, g_final):
    depth = w_in.shape[0]
    assert depth == 1, "the final norm is fused into the last layer's kernel; depth 1 only"
    layer = 0
    lw = dict(g_mix=g_mix[layer], w_in=w_in[layer], g_hgrn_norm=g_hgrn_norm[layer],
              conv_w=conv_w[layer], conv_b=conv_b[layer], w_rg_a=w_rg_a[layer], b_rg_a=b_rg_a[layer],
              w_rg_x=w_rg_x[layer], b_rg_x=b_rg_x[layer], lambda_rg=lambda_rg[layer],
              w_out=w_out[layer], g_ffn=g_ffn[layer], w_peer_q=w_peer_q[layer],
              peer_keys=peer_keys[layer], peer_u=peer_u[layer], peer_v=peer_v[layer],
              g_ple=g_ple[layer], w_ple_gate=w_ple_gate[layer], w_ple=w_ple[layer])
    bp = x_prompt.shape[0]
    y_p, s_p, r_p, c_p = _run_stream(
        x_prompt, p_prompt[layer], jnp.zeros((bp, H_A, DK, DV), F32), jnp.zeros((bp, W_B), F32),
        jnp.zeros((bp, CONV_W - 1, W_B), F32), lw, g_final, lower_bounds, layer)
    y_s, s_s, r_s, c_s = _run_stream(
        x_sample, p_sample[layer], state_hgrn[layer], state_rglru[layer], state_conv[layer],
        lw, g_final, lower_bounds, layer)
    return (y_p, y_s, s_p[None], r_p[None], c_p[None], s_s[None], r_s[None], c_s[None])
```

```python
import functools
import math

import jax
import jax.numpy as jnp
from jax import lax
from jax.experimental import pallas as pl
from jax.experimental.pallas import tpu as pltpu

D_MODEL = 1024
W_A = 512
DK = 128
DV = 128
H_A = W_A // DK
W_B = 512
CONV_W = 4
RG_C = 8.0
IN_COLS = 4 * W_A + 2 * W_B
CHUNK = 64
PEER_HEADS = 8
N_SUBKEYS = 128
PEER_TOPK = 16
PEER_DQ = 256
PEER_DHALF = PEER_DQ // 2
PEER_SEL = PEER_HEADS * PEER_TOPK
PLE_DIM = 256
EPS = 1e-6

LANES = 128
SUBLANES = 8
CONV_PAD = SUBLANES
ROW_TILES = D_MODEL // LANES
MIB = 1024 * 1024
VMEM_BYTES_V7X = 64 * MIB
VMEM_MIN_REQUEST = 16 * MIB

F32 = jnp.float32
BF16 = jnp.bfloat16


def _block(n, target):
    b = min(n, target)
    while n % b:
        b -= 1
    return b


def _params(semantics, block_bytes, scratch_bytes=0):
    need = 3 * block_bytes + scratch_bytes
    limit = min(max(need, VMEM_MIN_REQUEST), VMEM_BYTES_V7X // 8 * 7)
    return pltpu.CompilerParams(dimension_semantics=semantics, vmem_limit_bytes=limit)


def _f32_bytes(*shapes):
    return sum(4 * math.prod(s) for s in shapes)


def _rms(x, g):
    ms = jnp.mean(x * x, axis=-1, keepdims=True)
    return x * lax.rsqrt(ms + EPS) * g


def _sigmoid(x):
    return 1.0 / (1.0 + jnp.exp(-x))


def _gelu_tanh(x):
    c = math.sqrt(2.0 / math.pi)
    return x * (0.5 * (1.0 + jnp.tanh(c * (x + 0.044715 * (x * x * x)))))


def _bdot(a, b):
    return jnp.dot(a.astype(BF16), b.astype(BF16), preferred_element_type=F32)


def _bdot_nt(a, b):
    return lax.dot_general(a.astype(BF16), b.astype(BF16), (((1,), (1,)), ((), ())),
                           preferred_element_type=F32)


def _bdot_tn(a, b):
    return lax.dot_general(a.astype(BF16), b.astype(BF16), (((0,), (0,)), ((), ())),
                           preferred_element_type=F32)


def _cumsum_rows(x):
    n = x.shape[0]
    rows = lax.broadcasted_iota(jnp.int32, x.shape, 0)
    s = 1
    while s < n:
        x = x + jnp.where(rows >= s, pltpu.roll(x, s, 0), 0.0)
        s *= 2
    return x


def _linear_scan_rows(a, b):
    n = a.shape[0]
    rows = lax.broadcasted_iota(jnp.int32, a.shape, 0)
    s = 1
    while s < n:
        keep = rows >= s
        b = jnp.where(keep, a * pltpu.roll(b, s, 0) + b, b)
        a = jnp.where(keep, a * pltpu.roll(a, s, 0), a)
        s *= 2
    return a, b


def _hgrn_chunk(zq, zf, v, zg, lb, gn, tril, st_scr):
    chunk = zq.shape[0]
    f = lb + (1.0 - lb) * _sigmoid(zf)
    k = 1.0 - f
    q = zq * _sigmoid(zq)
    cum = _cumsum_rows(jnp.log(f))
    qd = q * jnp.exp(cum)
    kd = k * jnp.exp(-cum)
    last = cum[chunk - 1:chunk, :]
    kdec = k * jnp.exp(last - cum)
    dec = jnp.exp(last)
    outs = []
    for h in range(H_A):
        sl = slice(h * DK, (h + 1) * DK)
        att = jnp.where(tril, _bdot_nt(qd[:, sl], kd[:, sl]), 0.0)
        st = st_scr[h]
        o = _bdot(att, v[:, sl]) + _bdot_nt(qd[:, sl], st)
        st_scr[h] = st * dec[:, sl] + _bdot_tn(v[:, sl], kdec[:, sl])
        zgh = zg[:, sl]
        outs.append(_rms(o, gn[:, sl]) * (zgh * _sigmoid(zgh)))
    return outs


def _mixer_kernel(h_ref, gm_ref, win_ref, lbw_ref, gn_ref, cw_ref, cb_ref, wa_ref, ba_ref, wx_ref,
                  bx_ref, lam_ref, wout_ref, gf_ref, wq_ref, s0_ref, buf0_ref, r0_ref,
                  h1_ref, xn_ref, q_ref, s_out_ref, rlast_ref, bufnew_ref,
                  z_scr, o_scr, st_scr, xp_scr, hc_scr, *, tb, chunk, layer):
    t = pl.program_id(1)
    hist = CONV_W - 1

    @pl.when(t == 0)
    def _():
        for h in range(H_A):
            st_scr[h] = s0_ref[0, h].T
        xp_scr[CONV_PAD - hist:CONV_PAD, :] = buf0_ref[0]
        hc_scr[...] = r0_ref[0]

    hin = h_ref[0]
    z_scr[...] = _bdot(_rms(hin, gm_ref[...]), win_ref[...])

    lbw = lbw_ref[...]
    ex = jnp.exp(lbw - jnp.max(lbw, axis=0, keepdims=True))
    sm = ex / jnp.sum(ex, axis=0, keepdims=True)
    lb = jnp.sum(sm[0:layer + 1, :], axis=0, keepdims=True)
    gn = gn_ref[...]
    tril = (lax.broadcasted_iota(jnp.int32, (chunk, chunk), 0)
            >= lax.broadcasted_iota(jnp.int32, (chunk, chunk), 1))
    for c in range(tb // chunk):
        rows = pl.ds(c * chunk, chunk)
        zs = [z_scr[rows, j * W_A:(j + 1) * W_A] for j in range(4)]
        outs = _hgrn_chunk(zs[0], zs[1], zs[2], zs[3], lb, gn, tril, st_scr)
        for h in range(H_A):
            o_scr[rows, h * DK:(h + 1) * DK] = outs[h]

    x = z_scr[:, 4 * W_A:4 * W_A + W_B]
    xp_scr[CONV_PAD:CONV_PAD + tb, :] = x
    cw = cw_ref[...]
    acc = xp_scr[CONV_PAD - hist:CONV_PAD - hist + tb, :] * cw[0:1, :]
    for j in range(1, CONV_W):
        acc = acc + xp_scr[CONV_PAD - hist + j:CONV_PAD - hist + j + tb, :] * cw[j:j + 1, :]
    xc = cb_ref[...] + acc
    tail = xp_scr[CONV_PAD + tb - hist:CONV_PAD + tb, :]
    xp_scr[CONV_PAD - hist:CONV_PAD, :] = tail

    r = _sigmoid(_bdot(xc, wa_ref[...]) + ba_ref[...])
    gi = _sigmoid(_bdot(xc, wx_ref[...]) + bx_ref[...])
    nl = -lam_ref[...]
    softplus = jnp.maximum(nl, 0.0) + jnp.log1p(jnp.exp(-jnp.abs(nl)))
    log_a = -RG_C * r * softplus
    a = jnp.exp(log_a)
    one_minus_a2 = -jnp.tanh(log_a) * (a * a + 1.0)
    b = jnp.sqrt(one_minus_a2) * (gi * xc)
    aa, bb = _linear_scan_rows(a, b)
    hseq = bb + aa * hc_scr[...]
    hc_scr[...] = hseq[tb - 1:tb, :]
    o_scr[:, W_A:W_A + W_B] = hseq * _gelu_tanh(z_scr[:, 4 * W_A + W_B:4 * W_A + 2 * W_B])

    h1 = hin + _bdot(o_scr[...], wout_ref[...])
    xn = _rms(h1, gf_ref[...])
    for s in range(ROW_TILES):
        h1_ref[0, :, s, :] = h1[:, s * LANES:(s + 1) * LANES]
        xn_ref[0, :, s, :] = xn[:, s * LANES:(s + 1) * LANES]
    q_ref[0] = _bdot(xn, wq_ref[...])

    @pl.when(t == pl.num_programs(1) - 1)
    def _():
        for h in range(H_A):
            s_out_ref[0, h] = st_scr[h].T
        rlast_ref[0] = hseq[tb - 1:tb, :]
        bufnew_ref[0] = tail


def _block_diag(w):
    n, c, d = w.shape
    eye = jnp.eye(n, dtype=w.dtype)
    return (eye[:, None, :, None] * w[:, :, None, :]).reshape(n * c, n * d)


def _mixer(h, lw, lower_bounds, s0, buf0, r0, layer):
    B, L, _ = h.shape
    chunk = min(CHUNK, L)
    tb = chunk * _block(L // chunk, 8)
    nq = PEER_HEADS * PEER_DQ
    row = lambda v, w: v.reshape(1, w)
    full = lambda shape: pl.BlockSpec(shape, lambda b, t: (0,) * len(shape),
                                      pipeline_mode=pl.Buffered(1))
    per_batch = lambda shape: pl.BlockSpec((1,) + shape, lambda b, t: (b,) + (0,) * len(shape))
    tok3 = pl.BlockSpec((1, tb, ROW_TILES, LANES), lambda b, t: (b, t, 0, 0))
    kern = functools.partial(_mixer_kernel, tb=tb, chunk=chunk, layer=layer)
    h1, xn, q, s_new, r_new, c_new = pl.pallas_call(
        kern,
        out_shape=(jax.ShapeDtypeStruct((B, L, ROW_TILES, LANES), F32),
                   jax.ShapeDtypeStruct((B, L, ROW_TILES, LANES), F32),
                   jax.ShapeDtypeStruct((B, L, nq), F32),
                   jax.ShapeDtypeStruct((B, H_A, DK, DV), F32),
                   jax.ShapeDtypeStruct((B, 1, W_B), F32),
                   jax.ShapeDtypeStruct((B, CONV_W - 1, W_B), F32)),
        grid=(B, L // tb),
        in_specs=[pl.BlockSpec((1, tb, D_MODEL), lambda b, t: (b, t, 0)),
                  full((1, D_MODEL)), full((D_MODEL, IN_COLS)),
                  full(lower_bounds.shape), full((1, W_A)),
                  full((CONV_W, W_B)), full((1, W_B)),
                  full((W_B, W_B)), full((1, W_B)), full((W_B, W_B)), full((1, W_B)), full((1, W_B)),
                  full((W_A + W_B, D_MODEL)), full((1, D_MODEL)), full((D_MODEL, nq)),
                  per_batch((H_A, DK, DV)), per_batch((CONV_W - 1, W_B)), per_batch((1, W_B))],
        out_specs=(tok3, tok3,
                   pl.BlockSpec((1, tb, nq), lambda b, t: (b, t, 0)),
                   per_batch((H_A, DK, DV)), per_batch((1, W_B)), per_batch((CONV_W - 1, W_B))),
        scratch_shapes=[pltpu.VMEM((tb, IN_COLS), F32), pltpu.VMEM((tb, W_A + W_B), F32),
                        pltpu.VMEM((H_A, DV, DK), F32),
                        pltpu.VMEM((CONV_PAD + tb, W_B), F32), pltpu.VMEM((1, W_B), F32)],
        compiler_params=_params(
            ("arbitrary", "arbitrary"),
            _f32_bytes((tb, D_MODEL)) * 3 + _f32_bytes((tb, nq))
            + (D_MODEL * IN_COLS + 2 * W_B * W_B + (W_A + W_B) * D_MODEL + D_MODEL * nq),
            _f32_bytes((tb, IN_COLS), (tb, W_A + W_B), (H_A, DV, DK), (CONV_PAD + tb, W_B))),
        name="mixer",
    )(h, row(lw["g_mix"], D_MODEL), lw["w_in"].astype(BF16), lower_bounds,
      row(lw["g_hgrn_norm"], W_A), lw["conv_w"], row(lw["conv_b"], W_B),
      _block_diag(lw["w_rg_a"]).astype(BF16), row(lw["b_rg_a"], W_B),
      _block_diag(lw["w_rg_x"]).astype(BF16), row(lw["b_rg_x"], W_B), row(lw["lambda_rg"], W_B),
      lw["w_out"].astype(BF16), row(lw["g_ffn"], D_MODEL), lw["w_peer_q"].astype(BF16),
      s0, buf0, r0.reshape(B, 1, W_B))
    T = B * L
    return (h1.reshape(T, ROW_TILES, LANES), xn.reshape(T, ROW_TILES, LANES), q.reshape(T, nq),
            s_new, r_new.reshape(B, W_B), c_new)


def _topk_rows(vals, k, payload=None):
    n, w = vals.shape
    rows = lax.broadcasted_iota(jnp.int32, (n, w), 0).astype(F32)
    slot = lax.broadcasted_iota(jnp.int32, (k, w), 0)
    out_v = jnp.zeros((k, w), F32)
    out_p = jnp.zeros((k, w), F32)
    for i in range(k):
        m = jnp.max(vals, axis=0, keepdims=True)
        hit = jnp.min(jnp.where(vals == m, rows, float(n)), axis=0, keepdims=True)
        sel = rows == hit
        if payload is None:
            pick = hit
        else:
            pick = jnp.max(jnp.where(sel, payload, -1.0), axis=0, keepdims=True)
        out_v = jnp.where(slot == i, m, out_v)
        out_p = jnp.where(slot == i, pick, out_p)
        vals = jnp.where(sel, -jnp.inf, vals)
    return out_v, out_p


def _candidates(sv1, si1, sv2, si2):
    k, w = sv1.shape
    assert k == PEER_TOPK == 2 * SUBLANES
    row8 = lax.broadcasted_iota(jnp.int32, (SUBLANES, w), 0)
    vals = [sv1[0:1, :] + sv2, sv1[1:2, :] + sv2[0:SUBLANES, :]]
    idxs = [si1[0:1, :] * float(N_SUBKEYS) + si2, si1[1:2, :] * float(N_SUBKEYS) + si2[0:SUBLANES, :]]
    for i in range(2, SUBLANES):
        live = row8 < (k // (i + 1))
        vals.append(jnp.where(live, sv1[i:i + 1, :] + sv2[0:SUBLANES, :], -jnp.inf))
        idxs.append(si1[i:i + 1, :] * float(N_SUBKEYS) + si2[0:SUBLANES, :])
    vals.append(sv1[SUBLANES:k, :] + sv2[0:1, :])
    idxs.append(si1[SUBLANES:k, :] * float(N_SUBKEYS) + si2[0:1, :])
    return jnp.concatenate(vals, axis=0), jnp.concatenate(idxs, axis=0)


def _route_kernel(q_ref, keys_ref, idx_ref, gate_ref):
    def head(h, carry):
        col = pl.multiple_of(h * PEER_DQ, PEER_DQ)
        svs, sis = [], []
        for p in range(2):
            qs = q_ref[:, pl.ds(col + p * PEER_DHALF, PEER_DHALF)]
            s = _bdot_nt(keys_ref[h, p], qs)
            sv, si = _topk_rows(s, PEER_TOPK)
            svs.append(sv)
            sis.append(si)
        cand, cidx = _candidates(svs[0], sis[0], svs[1], sis[1])
        cs, ce = _topk_rows(cand, PEER_TOPK, payload=cidx)
        ex = jnp.exp(cs - cs[0:1, :])
        g = ex / jnp.sum(ex, axis=0, keepdims=True)
        row = pl.multiple_of(h * PEER_TOPK, PEER_TOPK)
        idx_ref[pl.ds(row, PEER_TOPK), :] = ce.astype(jnp.int32)
        gate_ref[pl.ds(row, PEER_TOPK), :] = g
        return carry

    lax.fori_loop(0, PEER_HEADS, head, 0)


def _route(q, keys):
    T = q.shape[0]
    tb = _block(T, 4 * LANES)
    nq = PEER_HEADS * PEER_DQ
    return pl.pallas_call(
        _route_kernel,
        out_shape=(jax.ShapeDtypeStruct((PEER_SEL, T), jnp.int32),
                   jax.ShapeDtypeStruct((PEER_SEL, T), F32)),
        grid=(T // tb,),
        in_specs=[pl.BlockSpec((tb, nq), lambda i: (i, 0)),
                  pl.BlockSpec(keys.shape, lambda i: (0, 0, 0, 0))],
        out_specs=(pl.BlockSpec((PEER_SEL, tb), lambda i: (0, i)),
                   pl.BlockSpec((PEER_SEL, tb), lambda i: (0, i))),
        compiler_params=_params(
            ("arbitrary",), _f32_bytes((tb, nq), (2 * PEER_SEL, tb)) + 2 * math.prod(keys.shape)),
        name="route",
    )(q, keys.astype(BF16))


GATHER_TOKENS = 128
GATHER_SLOTS = 8
DMA_PRIORITIES = 2


ACC_CHAINS = 2
ISSUE_LAG = 6


def _sum_sublanes_of_each(ps):
    assert len(ps) == SUBLANES == 8
    sub = lax.broadcasted_iota(jnp.int32, ps[0].shape, 0)
    low = sub < 4
    r = []
    for j in range(4):
        a, b = ps[j], ps[j + 4]
        r.append(jnp.where(low, a, b) + pltpu.roll(jnp.where(low, b, a), 4, 0))
    even2 = (sub & 2) == 0
    z = [jnp.where(even2, r[j] + pltpu.roll(r[j], 6, 0), r[j + 2] + pltpu.roll(r[j + 2], 2, 0))
         for j in range(2)]
    even1 = (sub & 1) == 0
    return jnp.where(even1, z[0] + pltpu.roll(z[0], 7, 0), z[1] + pltpu.roll(z[1], 1, 0))


def _gather_kernel(idx_ref, idx_next_ref, xn_ref, h1_ref, gate_ref, uv_hbm, out_ref, uvbuf, sem,
                   *, tbg, steps_per_gate_block):
    step = pl.program_id(0)
    n_steps = pl.num_programs(0)
    lane0 = (step % steps_per_gate_block) * tbg
    ahead = GATHER_SLOTS - 1

    def issue(ids_ref, base, slot, k_lo=0, k_hi=PEER_SEL):
        ids = ids_ref.at[0, 0, pl.ds(base, PEER_SEL)]
        for k in range(k_lo, k_hi):
            pltpu.make_async_copy(uv_hbm.at[ids[k]], uvbuf.at[slot, k],
                                  sem.at[slot]).start(priority=k % DMA_PRIORITIES)

    def wait(slot):
        pltpu.make_async_copy(uvbuf.at[slot], uvbuf.at[slot], sem.at[slot]).wait()

    lanes = lax.broadcasted_iota(jnp.int32, (SUBLANES, gate_ref.shape[1]), 1)

    def token(tok, ids_ref, ids_base):
        slot = tok & (GATHER_SLOTS - 1)
        slot_ahead = (tok + ahead) & (GATHER_SLOTS - 1)
        wait(slot)
        x8 = xn_ref[tok]
        lane = lane0 + tok
        accs = [None] * ACC_CHAINS
        n_grp = PEER_SEL // SUBLANES
        for grp in range(n_grp):
            k0 = grp * SUBLANES
            prods = [uvbuf[slot, k0 + j].astype(F32)[0:ROW_TILES, :] * x8 for j in range(SUBLANES)]
            g8 = jnp.sum(jnp.where(lanes == lane, gate_ref[k0:k0 + SUBLANES, :], 0.0),
                         axis=1, keepdims=True)
            chunks = [grp - ISSUE_LAG] if grp < n_grp - 1 else range(grp - ISSUE_LAG, n_grp)
            for c in chunks:
                if c >= 0:
                    issue(ids_ref, ids_base, slot_ahead, c * SUBLANES, (c + 1) * SUBLANES)
            act = jnp.sum(_sum_sublanes_of_each(prods), axis=1, keepdims=True)
            w = jnp.broadcast_to(g8 * _gelu_tanh(act), (SUBLANES, LANES))
            for j in range(SUBLANES):
                term = (jnp.broadcast_to(w[j:j + 1, :], (ROW_TILES, LANES))
                        * uvbuf[slot, k0 + j].astype(F32)[ROW_TILES:2 * ROW_TILES, :])
                c = (k0 + j) % ACC_CHAINS
                accs[c] = term if accs[c] is None else accs[c] + term
        out_ref[tok] = h1_ref[tok] + functools.reduce(lambda a, b: a + b, accs)

    @pl.when(step == 0)
    def _():
        for s in range(ahead):
            issue(idx_ref, s * PEER_SEL, s)

    def body_same_step(tok, carry):
        token(tok, idx_ref, (tok + ahead) * PEER_SEL)
        return carry

    def body_next_step(tok, carry):
        token(tok, idx_next_ref, (tok + ahead - tbg) * PEER_SEL)
        return carry

    lax.fori_loop(0, tbg - ahead, body_same_step, 0)
    lax.fori_loop(tbg - ahead, tbg, body_next_step, 0)

    @pl.when(step == n_steps - 1)
    def _():
        for s in range(ahead):
            wait(s)


def _gather(eidx, gates, xn, h1, peer_u, peer_v):
    T = xn.shape[0]
    tbg = _block(T, GATHER_TOKENS)
    assert tbg % GATHER_SLOTS == 0 and GATHER_SLOTS & (GATHER_SLOTS - 1) == 0
    n_steps = T // tbg
    ahead = GATHER_SLOTS - 1
    gate_tb = _block(T, LANES)
    per = gate_tb // tbg
    n_exp = peer_u.shape[0]
    uv = jnp.concatenate([peer_u.reshape(n_exp, ROW_TILES, LANES),
                          peer_v.reshape(n_exp, ROW_TILES, LANES)], axis=1).astype(BF16)
    idx = eidx.T.reshape(n_steps, 1, tbg * PEER_SEL)
    idx_head = idx[:, :, :ahead * PEER_SEL]
    kern = functools.partial(_gather_kernel, tbg=tbg, steps_per_gate_block=per)
    tok_spec = pl.BlockSpec((tbg, ROW_TILES, LANES), lambda i: (i, 0, 0))
    out = pl.pallas_call(
        kern,
        out_shape=jax.ShapeDtypeStruct((T, ROW_TILES, LANES), F32),
        grid=(n_steps,),
        in_specs=[pl.BlockSpec((1, 1, tbg * PEER_SEL), lambda i: (i, 0, 0),
                               memory_space=pltpu.SMEM),
                  pl.BlockSpec((1, 1, ahead * PEER_SEL),
                               lambda i: (jnp.minimum(i + 1, n_steps - 1), 0, 0),
                               memory_space=pltpu.SMEM),
                  tok_spec, tok_spec,
                  pl.BlockSpec((PEER_SEL, gate_tb), lambda i: (0, i // per)),
                  pl.BlockSpec(memory_space=pl.ANY)],
        out_specs=tok_spec,
        scratch_shapes=[pltpu.VMEM((GATHER_SLOTS, PEER_SEL, 2 * ROW_TILES, LANES), BF16),
                        pltpu.SemaphoreType.DMA((GATHER_SLOTS,))],
        compiler_params=_params(
            ("arbitrary",), _f32_bytes((tbg, D_MODEL)) * 3 + _f32_bytes((PEER_SEL, gate_tb)),
            _f32_bytes((GATHER_SLOTS, PEER_SEL, D_MODEL))),
        name="gather",
    )(idx, idx_head, xn, h1, gates, uv)
    return out


def _ple_kernel(h_ref, p_ref, wp_ref, gp_ref, wg_ref, gf_ref, y_ref):
    h2 = jnp.concatenate([h_ref[:, s, :] for s in range(ROW_TILES)], axis=1)
    gate = _sigmoid(_bdot(_rms(h2, gp_ref[...]), wg_ref[...]))
    h3 = h2 + _bdot(p_ref[...], wp_ref[...]) * gate
    y_ref[...] = _rms(h3, gf_ref[...])


def _ple(h2, p2d, w_ple, g_ple, w_gate, g_final):
    T = h2.shape[0]
    tb = _block(T, 512)
    tok = lambda w: pl.BlockSpec((tb, w), lambda i: (i, 0))
    full = lambda shape: pl.BlockSpec(shape, lambda i: (0, 0))
    return pl.pallas_call(
        _ple_kernel,
        out_shape=jax.ShapeDtypeStruct((T, D_MODEL), F32),
        grid=(T // tb,),
        in_specs=[pl.BlockSpec((tb, ROW_TILES, LANES), lambda i: (i, 0, 0)),
                  tok(PLE_DIM), full((PLE_DIM, D_MODEL)), full((1, D_MODEL)),
                  full((D_MODEL, D_MODEL)), full((1, D_MODEL))],
        out_specs=tok(D_MODEL),
        compiler_params=_params(
            ("arbitrary",), _f32_bytes((tb, D_MODEL)) * 2 + _f32_bytes((tb, PLE_DIM))
            + 2 * (PLE_DIM * D_MODEL + D_MODEL * D_MODEL)),
        name="ple",
    )(h2, p2d, w_ple.astype(BF16), g_ple.reshape(1, D_MODEL), w_gate.astype(BF16),
      g_final.reshape(1, D_MODEL))


def _run_stream(h, p, s0, r0, c0, lw, g_final, lower_bounds, layer):
    B, L, _ = h.shape
    T = B * L
    h1, xn, q, s_new, r_new, c_new = _mixer(h, lw, lower_bounds, s0, c0, r0, layer)
    eidx, gates = _route(q, lw["peer_keys"])
    h2 = _gather(eidx, gates, xn, h1, lw["peer_u"], lw["peer_v"])
    y = _ple(h2, p.reshape(T, PLE_DIM), lw["w_ple"], lw["g_ple"], lw["w_ple_gate"], g_final)
    return y.reshape(B, L, D_MODEL), s_new, r_new, c_new


def kernel(x_prompt, x_sample, state_hgrn, state_rglru, state_conv, p_prompt, p_sample, g_mix, w_in, lower_bounds, g_hgrn_norm, conv_w, conv_b, w_rg_a, b_rg_a, w_rg_x, b_rg_x, lambda_rg, w_out, g_ffn, w_peer_q, peer_keys, peer_u, peer_v, g_ple, w_ple_gate, w_ple, g_final):
    depth = w_in.shape[0]
    assert depth == 1, "the final norm is fused into the last layer's kernel; depth 1 only"
    layer = 0
    lw = dict(g_mix=g_mix[layer], w_in=w_in[layer], g_hgrn_norm=g_hgrn_norm[layer],
              conv_w=conv_w[layer], conv_b=conv_b[layer], w_rg_a=w_rg_a[layer], b_rg_a=b_rg_a[layer],
              w_rg_x=w_rg_x[layer], b_rg_x=b_rg_x[layer], lambda_rg=lambda_rg[layer],
              w_out=w_out[layer], g_ffn=g_ffn[layer], w_peer_q=w_peer_q[layer],
              peer_keys=peer_keys[layer], peer_u=peer_u[layer], peer_v=peer_v[layer],
              g_ple=g_ple[layer], w_ple_gate=w_ple_gate[layer], w_ple=w_ple[layer])
    bp = x_prompt.shape[0]
    y_p, s_p, r_p, c_p = _run_stream(
        x_prompt, p_prompt[layer], jnp.zeros((bp, H_A, DK, DV), F32), jnp.zeros((bp, W_B), F32),
        jnp.zeros((bp, CONV_W - 1, W_B), F32), lw, g_final, lower_bounds, layer)
    y_s, s_s, r_s, c_s = _run_stream(
        x_sample, p_sample[layer], state_hgrn[layer], state_rglru[layer], state_conv[layer],
        lw, g_final, lower_bounds, layer)
    return (y_p, y_s, s_p[None], r_p[None], c_p[None], s_s[None], r_s[None], c_s[None])
```

```python
import functools
import math

import jax
import jax.numpy as jnp
from jax import lax
from jax.experimental import pallas as pl
from jax.experimental.pallas import tpu as pltpu

D_MODEL = 1024
W_A = 512
DK = 128
DV = 128
H_A = W_A // DK
W_B = 512
CONV_W = 4
RG_C = 8.0
IN_COLS = 4 * W_A + 2 * W_B
CHUNK = 64
PEER_HEADS = 8
N_SUBKEYS = 128
PEER_TOPK = 16
PEER_DQ = 256
PEER_DHALF = PEER_DQ // 2
PEER_SEL = PEER_HEADS * PEER_TOPK
PLE_DIM = 256
EPS = 1e-6

LANES = 128
SUBLANES = 8
CONV_PAD = SUBLANES
ROW_TILES = D_MODEL // LANES
MIB = 1024 * 1024
VMEM_BYTES_V7X = 64 * MIB
VMEM_MIN_REQUEST = 16 * MIB

F32 = jnp.float32
BF16 = jnp.bfloat16


def _block(n, target):
    b = min(n, target)
    while n % b:
        b -= 1
    return b


def _params(semantics, block_bytes, scratch_bytes=0):
    need = 3 * block_bytes + scratch_bytes
    limit = min(max(need, VMEM_MIN_REQUEST), VMEM_BYTES_V7X // 8 * 7)
    return pltpu.CompilerParams(dimension_semantics=semantics, vmem_limit_bytes=limit)


def _f32_bytes(*shapes):
    return sum(4 * math.prod(s) for s in shapes)


def _rms(x, g):
    ms = jnp.mean(x * x, axis=-1, keepdims=True)
    return x * lax.rsqrt(ms + EPS) * g


def _sigmoid(x):
    return 1.0 / (1.0 + jnp.exp(-x))


def _gelu_tanh(x):
    c = math.sqrt(2.0 / math.pi)
    return x * (0.5 * (1.0 + jnp.tanh(c * (x + 0.044715 * (x * x * x)))))


def _bdot(a, b):
    return jnp.dot(a.astype(BF16), b.astype(BF16), preferred_element_type=F32)


def _bdot_nt(a, b):
    return lax.dot_general(a.astype(BF16), b.astype(BF16), (((1,), (1,)), ((), ())),
                           preferred_element_type=F32)


def _bdot_tn(a, b):
    return lax.dot_general(a.astype(BF16), b.astype(BF16), (((0,), (0,)), ((), ())),
                           preferred_element_type=F32)


def _cumsum_rows(x):
    n = x.shape[0]
    rows = lax.broadcasted_iota(jnp.int32, x.shape, 0)
    s = 1
    while s < n:
        x = x + jnp.where(rows >= s, pltpu.roll(x, s, 0), 0.0)
        s *= 2
    return x


def _linear_scan_rows(a, b):
    n = a.shape[0]
    rows = lax.broadcasted_iota(jnp.int32, a.shape, 0)
    s = 1
    while s < n:
        keep = rows >= s
        b = jnp.where(keep, a * pltpu.roll(b, s, 0) + b, b)
        a = jnp.where(keep, a * pltpu.roll(a, s, 0), a)
        s *= 2
    return a, b


def _hgrn_chunk(zq, zf, v, zg, lb, gn, tril, st_scr):
    chunk = zq.shape[0]
    f = lb + (1.0 - lb) * _sigmoid(zf)
    k = 1.0 - f
    q = zq * _sigmoid(zq)
    cum = _cumsum_rows(jnp.log(f))
    qd = q * jnp.exp(cum)
    kd = k * jnp.exp(-cum)
    last = cum[chunk - 1:chunk, :]
    kdec = k * jnp.exp(last - cum)
    dec = jnp.exp(last)
    outs = []
    for h in range(H_A):
        sl = slice(h * DK, (h + 1) * DK)
        att = jnp.where(tril, _bdot_nt(qd[:, sl], kd[:, sl]), 0.0)
        st = st_scr[h]
        o = _bdot(att, v[:, sl]) + _bdot_nt(qd[:, sl], st)
        st_scr[h] = st * dec[:, sl] + _bdot_tn(v[:, sl], kdec[:, sl])
        zgh = zg[:, sl]
        outs.append(_rms(o, gn[:, sl]) * (zgh * _sigmoid(zgh)))
    return outs


def _mixer_kernel(h_ref, gm_ref, win_ref, lbw_ref, gn_ref, cw_ref, cb_ref, wa_ref, ba_ref, wx_ref,
                  bx_ref, lam_ref, wout_ref, gf_ref, wq_ref, s0_ref, buf0_ref, r0_ref,
                  h1_ref, xn_ref, q_ref, s_out_ref, rlast_ref, bufnew_ref,
                  z_scr, o_scr, st_scr, xp_scr, hc_scr, *, tb, chunk, layer):
    t = pl.program_id(1)
    hist = CONV_W - 1

    @pl.when(t == 0)
    def _():
        for h in range(H_A):
            st_scr[h] = s0_ref[0, h].T
        xp_scr[CONV_PAD - hist:CONV_PAD, :] = buf0_ref[0]
        hc_scr[...] = r0_ref[0]

    hin = h_ref[0]
    z_scr[...] = _bdot(_rms(hin, gm_ref[...]), win_ref[...])

    lbw = lbw_ref[...]
    ex = jnp.exp(lbw - jnp.max(lbw, axis=0, keepdims=True))
    sm = ex / jnp.sum(ex, axis=0, keepdims=True)
    lb = jnp.sum(sm[0:layer + 1, :], axis=0, keepdims=True)
    gn = gn_ref[...]
    tril = (lax.broadcasted_iota(jnp.int32, (chunk, chunk), 0)
            >= lax.broadcasted_iota(jnp.int32, (chunk, chunk), 1))
    for c in range(tb // chunk):
        rows = pl.ds(c * chunk, chunk)
        zs = [z_scr[rows, j * W_A:(j + 1) * W_A] for j in range(4)]
        outs = _hgrn_chunk(zs[0], zs[1], zs[2], zs[3], lb, gn, tril, st_scr)
        for h in range(H_A):
            o_scr[rows, h * DK:(h + 1) * DK] = outs[h]

    x = z_scr[:, 4 * W_A:4 * W_A + W_B]
    xp_scr[CONV_PAD:CONV_PAD + tb, :] = x
    cw = cw_ref[...]
    acc = xp_scr[CONV_PAD - hist:CONV_PAD - hist + tb, :] * cw[0:1, :]
    for j in range(1, CONV_W):
        acc = acc + xp_scr[CONV_PAD - hist + j:CONV_PAD - hist + j + tb, :] * cw[j:j + 1, :]
    xc = cb_ref[...] + acc
    tail = xp_scr[CONV_PAD + tb - hist:CONV_PAD + tb, :]
    xp_scr[CONV_PAD - hist:CONV_PAD, :] = tail

    r = _sigmoid(_bdot(xc, wa_ref[...]) + ba_ref[...])
    gi = _sigmoid(_bdot(xc, wx_ref[...]) + bx_ref[...])
    nl = -lam_ref[...]
    softplus = jnp.maximum(nl, 0.0) + jnp.log1p(jnp.exp(-jnp.abs(nl)))
    log_a = -RG_C * r * softplus
    a = jnp.exp(log_a)
    one_minus_a2 = -jnp.tanh(log_a) * (a * a + 1.0)
    b = jnp.sqrt(one_minus_a2) * (gi * xc)
    aa, bb = _linear_scan_rows(a, b)
    hseq = bb + aa * hc_scr[...]
    hc_scr[...] = hseq[tb - 1:tb, :]
    o_scr[:, W_A:W_A + W_B] = hseq * _gelu_tanh(z_scr[:, 4 * W_A + W_B:4 * W_A + 2 * W_B])

    h1 = hin + _bdot(o_scr[...], wout_ref[...])
    xn = _rms(h1, gf_ref[...])
    for s in range(ROW_TILES):
        h1_ref[0, :, s, :] = h1[:, s * LANES:(s + 1) * LANES]
        xn_ref[0, :, s, :] = xn[:, s * LANES:(s + 1) * LANES]
    q_ref[0] = _bdot(xn, wq_ref[...])

    @pl.when(t == pl.num_programs(1) - 1)
    def _():
        for h in range(H_A):
            s_out_ref[0, h] = st_scr[h].T
        rlast_ref[0] = hseq[tb - 1:tb, :]
        bufnew_ref[0] = tail


def _block_diag(w):
    n, c, d = w.shape
    eye = jnp.eye(n, dtype=w.dtype)
    return (eye[:, None, :, None] * w[:, :, None, :]).reshape(n * c, n * d)


def _mixer(h, lw, lower_bounds, s0, buf0, r0, layer):
    B, L, _ = h.shape
    chunk = min(CHUNK, L)
    tb = chunk * _block(L // chunk, 8)
    nq = PEER_HEADS * PEER_DQ
    row = lambda v, w: v.reshape(1, w)
    full = lambda shape: pl.BlockSpec(shape, lambda b, t: (0,) * len(shape),
                                      pipeline_mode=pl.Buffered(1))
    per_batch = lambda shape: pl.BlockSpec((1,) + shape, lambda b, t: (b,) + (0,) * len(shape))
    tok3 = pl.BlockSpec((1, tb, ROW_TILES, LANES), lambda b, t: (b, t, 0, 0))
    kern = functools.partial(_mixer_kernel, tb=tb, chunk=chunk, layer=layer)
    h1, xn, q, s_new, r_new, c_new = pl.pallas_call(
        kern,
        out_shape=(jax.ShapeDtypeStruct((B, L, ROW_TILES, LANES), F32),
                   jax.ShapeDtypeStruct((B, L, ROW_TILES, LANES), F32),
                   jax.ShapeDtypeStruct((B, L, nq), F32),
                   jax.ShapeDtypeStruct((B, H_A, DK, DV), F32),
                   jax.ShapeDtypeStruct((B, 1, W_B), F32),
                   jax.ShapeDtypeStruct((B, CONV_W - 1, W_B), F32)),
        grid=(B, L // tb),
        in_specs=[pl.BlockSpec((1, tb, D_MODEL), lambda b, t: (b, t, 0)),
                  full((1, D_MODEL)), full((D_MODEL, IN_COLS)),
                  full(lower_bounds.shape), full((1, W_A)),
                  full((CONV_W, W_B)), full((1, W_B)),
                  full((W_B, W_B)), full((1, W_B)), full((W_B, W_B)), full((1, W_B)), full((1, W_B)),
                  full((W_A + W_B, D_MODEL)), full((1, D_MODEL)), full((D_MODEL, nq)),
                  per_batch((H_A, DK, DV)), per_batch((CONV_W - 1, W_B)), per_batch((1, W_B))],
        out_specs=(tok3, tok3,
                   pl.BlockSpec((1, tb, nq), lambda b, t: (b, t, 0)),
                   per_batch((H_A, DK, DV)), per_batch((1, W_B)), per_batch((CONV_W - 1, W_B))),
        scratch_shapes=[pltpu.VMEM((tb, IN_COLS), F32), pltpu.VMEM((tb, W_A + W_B), F32),
                        pltpu.VMEM((H_A, DV, DK), F32),
                        pltpu.VMEM((CONV_PAD + tb, W_B), F32), pltpu.VMEM((1, W_B), F32)],
        compiler_params=_params(
            ("arbitrary", "arbitrary"),
            _f32_bytes((tb, D_MODEL)) * 3 + _f32_bytes((tb, nq))
            + (D_MODEL * IN_COLS + 2 * W_B * W_B + (W_A + W_B) * D_MODEL + D_MODEL * nq),
            _f32_bytes((tb, IN_COLS), (tb, W_A + W_B), (H_A, DV, DK), (CONV_PAD + tb, W_B))),
        name="mixer",
    )(h, row(lw["g_mix"], D_MODEL), lw["w_in"].astype(BF16), lower_bounds,
      row(lw["g_hgrn_norm"], W_A), lw["conv_w"], row(lw["conv_b"], W_B),
      _block_diag(lw["w_rg_a"]).astype(BF16), row(lw["b_rg_a"], W_B),
      _block_diag(lw["w_rg_x"]).astype(BF16), row(lw["b_rg_x"], W_B), row(lw["lambda_rg"], W_B),
      lw["w_out"].astype(BF16), row(lw["g_ffn"], D_MODEL), lw["w_peer_q"].astype(BF16),
      s0, buf0, r0.reshape(B, 1, W_B))
    T = B * L
    return (h1.reshape(T, ROW_TILES, LANES), xn.reshape(T, ROW_TILES, LANES), q.reshape(T, nq),
            s_new, r_new.reshape(B, W_B), c_new)


def _topk_rows(vals, k, payload=None):
    n, w = vals.shape
    rows = lax.broadcasted_iota(jnp.int32, (n, w), 0).astype(F32)
    slot = lax.broadcasted_iota(jnp.int32, (k, w), 0)
    out_v = jnp.zeros((k, w), F32)
    out_p = jnp.zeros((k, w), F32)
    for i in range(k):
        m = jnp.max(vals, axis=0, keepdims=True)
        hit = jnp.min(jnp.where(vals == m, rows, float(n)), axis=0, keepdims=True)
        sel = rows == hit
        if payload is None:
            pick = hit
        else:
            pick = jnp.max(jnp.where(sel, payload, -1.0), axis=0, keepdims=True)
        out_v = jnp.where(slot == i, m, out_v)
        out_p = jnp.where(slot == i, pick, out_p)
        vals = jnp.where(sel, -jnp.inf, vals)
    return out_v, out_p


def _candidates(sv1, si1, sv2, si2):
    k, w = sv1.shape
    assert k == PEER_TOPK == 2 * SUBLANES
    row8 = lax.broadcasted_iota(jnp.int32, (SUBLANES, w), 0)
    vals = [sv1[0:1, :] + sv2, sv1[1:2, :] + sv2[0:SUBLANES, :]]
    idxs = [si1[0:1, :] * float(N_SUBKEYS) + si2, si1[1:2, :] * float(N_SUBKEYS) + si2[0:SUBLANES, :]]
    for i in range(2, SUBLANES):
        live = row8 < (k // (i + 1))
        vals.append(jnp.where(live, sv1[i:i + 1, :] + sv2[0:SUBLANES, :], -jnp.inf))
        idxs.append(si1[i:i + 1, :] * float(N_SUBKEYS) + si2[0:SUBLANES, :])
    vals.append(sv1[SUBLANES:k, :] + sv2[0:1, :])
    idxs.append(si1[SUBLANES:k, :] * float(N_SUBKEYS) + si2[0:1, :])
    return jnp.concatenate(vals, axis=0), jnp.concatenate(idxs, axis=0)


def _route_kernel(q_ref, keys_ref, idx_ref, gate_ref):
    def head(h, carry):
        col = pl.multiple_of(h * PEER_DQ, PEER_DQ)
        svs, sis = [], []
        for p in range(2):
            qs = q_ref[:, pl.ds(col + p * PEER_DHALF, PEER_DHALF)]
            s = _bdot_nt(keys_ref[h, p], qs)
            sv, si = _topk_rows(s, PEER_TOPK)
            svs.append(sv)
            sis.append(si)
        cand, cidx = _candidates(svs[0], sis[0], svs[1], sis[1])
        cs, ce = _topk_rows(cand, PEER_TOPK, payload=cidx)
        ex = jnp.exp(cs - cs[0:1, :])
        g = ex / jnp.sum(ex, axis=0, keepdims=True)
        row = pl.multiple_of(h * PEER_TOPK, PEER_TOPK)
        idx_ref[pl.ds(row, PEER_TOPK), :] = ce.astype(jnp.int32)
        gate_ref[pl.ds(row, PEER_TOPK), :] = g
        return carry

    lax.fori_loop(0, PEER_HEADS, head, 0)


def _route(q, keys):
    T = q.shape[0]
    tb = _block(T, 8 * LANES)
    nq = PEER_HEADS * PEER_DQ
    return pl.pallas_call(
        _route_kernel,
        out_shape=(jax.ShapeDtypeStruct((PEER_SEL, T), jnp.int32),
                   jax.ShapeDtypeStruct((PEER_SEL, T), F32)),
        grid=(T // tb,),
        in_specs=[pl.BlockSpec((tb, nq), lambda i: (i, 0)),
                  pl.BlockSpec(keys.shape, lambda i: (0, 0, 0, 0))],
        out_specs=(pl.BlockSpec((PEER_SEL, tb), lambda i: (0, i)),
                   pl.BlockSpec((PEER_SEL, tb), lambda i: (0, i))),
        compiler_params=_params(
            ("arbitrary",), _f32_bytes((tb, nq), (2 * PEER_SEL, tb)) + 2 * math.prod(keys.shape)),
        name="route",
    )(q, keys.astype(BF16))


GATHER_TOKENS = 128
GATHER_SLOTS = 8
DMA_PRIORITIES = 2


ACC_CHAINS = 2
ISSUE_LAG = 6


def _sum_sublanes_of_each(ps):
    assert len(ps) == SUBLANES == 8
    sub = lax.broadcasted_iota(jnp.int32, ps[0].shape, 0)
    low = sub < 4
    r = []
    for j in range(4):
        a, b = ps[j], ps[j + 4]
        r.append(jnp.where(low, a, b) + pltpu.roll(jnp.where(low, b, a), 4, 0))
    even2 = (sub & 2) == 0
    z = [jnp.where(even2, r[j] + pltpu.roll(r[j], 6, 0), r[j + 2] + pltpu.roll(r[j + 2], 2, 0))
         for j in range(2)]
    even1 = (sub & 1) == 0
    return jnp.where(even1, z[0] + pltpu.roll(z[0], 7, 0), z[1] + pltpu.roll(z[1], 1, 0))


def _gather_kernel(idx_ref, idx_next_ref, xn_ref, h1_ref, gate_ref, uv_hbm, out_ref, uvbuf, sem,
                   *, tbg, steps_per_gate_block):
    step = pl.program_id(0)
    n_steps = pl.num_programs(0)
    lane0 = (step % steps_per_gate_block) * tbg
    ahead = GATHER_SLOTS - 1

    def issue(ids_ref, base, slot, k_lo=0, k_hi=PEER_SEL):
        ids = ids_ref.at[0, 0, pl.ds(base, PEER_SEL)]
        for k in range(k_lo, k_hi):
            pltpu.make_async_copy(uv_hbm.at[ids[k]], uvbuf.at[slot, k],
                                  sem.at[slot]).start(priority=k % DMA_PRIORITIES)

    def wait(slot):
        pltpu.make_async_copy(uvbuf.at[slot], uvbuf.at[slot], sem.at[slot]).wait()

    lanes = lax.broadcasted_iota(jnp.int32, (SUBLANES, gate_ref.shape[1]), 1)

    def token(tok, ids_ref, ids_base):
        slot = tok & (GATHER_SLOTS - 1)
        slot_ahead = (tok + ahead) & (GATHER_SLOTS - 1)
        wait(slot)
        x8 = xn_ref[tok]
        lane = lane0 + tok
        accs = [None] * ACC_CHAINS
        n_grp = PEER_SEL // SUBLANES
        for grp in range(n_grp):
            k0 = grp * SUBLANES
            prods = [uvbuf[slot, k0 + j].astype(F32)[0:ROW_TILES, :] * x8 for j in range(SUBLANES)]
            g8 = jnp.sum(jnp.where(lanes == lane, gate_ref[k0:k0 + SUBLANES, :], 0.0),
                         axis=1, keepdims=True)
            chunks = [grp - ISSUE_LAG] if grp < n_grp - 1 else range(grp - ISSUE_LAG, n_grp)
            for c in chunks:
                if c >= 0:
                    issue(ids_ref, ids_base, slot_ahead, c * SUBLANES, (c + 1) * SUBLANES)
            act = jnp.sum(_sum_sublanes_of_each(prods), axis=1, keepdims=True)
            w = jnp.broadcast_to(g8 * _gelu_tanh(act), (SUBLANES, LANES))
            for j in range(SUBLANES):
                term = (jnp.broadcast_to(w[j:j + 1, :], (ROW_TILES, LANES))
                        * uvbuf[slot, k0 + j].astype(F32)[ROW_TILES:2 * ROW_TILES, :])
                c = (k0 + j) % ACC_CHAINS
                accs[c] = term if accs[c] is None else accs[c] + term
        out_ref[tok] = h1_ref[tok] + functools.reduce(lambda a, b: a + b, accs)

    @pl.when(step == 0)
    def _():
        for s in range(ahead):
            issue(idx_ref, s * PEER_SEL, s)

    def body_same_step(tok, carry):
        token(tok, idx_ref, (tok + ahead) * PEER_SEL)
        return carry

    def body_next_step(tok, carry):
        token(tok, idx_next_ref, (tok + ahead - tbg) * PEER_SEL)
        return carry

    lax.fori_loop(0, tbg - ahead, body_same_step, 0)
    lax.fori_loop(tbg - ahead, tbg, body_next_step, 0)

    @pl.when(step == n_steps - 1)
    def _():
        for s in range(ahead):
            wait(s)


def _gather(eidx, gates, xn, h1, peer_u, peer_v):
    T = xn.shape[0]
    tbg = _block(T, GATHER_TOKENS)
    assert tbg % GATHER_SLOTS == 0 and GATHER_SLOTS & (GATHER_SLOTS - 1) == 0
    n_steps = T // tbg
    ahead = GATHER_SLOTS - 1
    gate_tb = _block(T, LANES)
    per = gate_tb // tbg
    n_exp = peer_u.shape[0]
    uv = jnp.concatenate([peer_u.reshape(n_exp, ROW_TILES, LANES),
                          peer_v.reshape(n_exp, ROW_TILES, LANES)], axis=1).astype(BF16)
    idx = eidx.T.reshape(n_steps, 1, tbg * PEER_SEL)
    idx_head = idx[:, :, :ahead * PEER_SEL]
    kern = functools.partial(_gather_kernel, tbg=tbg, steps_per_gate_block=per)
    tok_spec = pl.BlockSpec((tbg, ROW_TILES, LANES), lambda i: (i, 0, 0))
    out = pl.pallas_call(
        kern,
        out_shape=jax.ShapeDtypeStruct((T, ROW_TILES, LANES), F32),
        grid=(n_steps,),
        in_specs=[pl.BlockSpec((1, 1, tbg * PEER_SEL), lambda i: (i, 0, 0),
                               memory_space=pltpu.SMEM),
                  pl.BlockSpec((1, 1, ahead * PEER_SEL),
                               lambda i: (jnp.minimum(i + 1, n_steps - 1), 0, 0),
                               memory_space=pltpu.SMEM),
                  tok_spec, tok_spec,
                  pl.BlockSpec((PEER_SEL, gate_tb), lambda i: (0, i // per)),
                  pl.BlockSpec(memory_space=pl.ANY)],
        out_specs=tok_spec,
        scratch_shapes=[pltpu.VMEM((GATHER_SLOTS, PEER_SEL, 2 * ROW_TILES, LANES), BF16),
                        pltpu.SemaphoreType.DMA((GATHER_SLOTS,))],
        compiler_params=_params(
            ("arbitrary",), _f32_bytes((tbg, D_MODEL)) * 3 + _f32_bytes((PEER_SEL, gate_tb)),
            _f32_bytes((GATHER_SLOTS, PEER_SEL, D_MODEL))),
        name="gather",
    )(idx, idx_head, xn, h1, gates, uv)
    return out


def _ple_kernel(h_ref, p_ref, wp_ref, gp_ref, wg_ref, gf_ref, y_ref):
    h2 = jnp.concatenate([h_ref[:, s, :] for s in range(ROW_TILES)], axis=1)
    gate = _sigmoid(_bdot(_rms(h2, gp_ref[...]), wg_ref[...]))
    h3 = h2 + _bdot(p_ref[...], wp_ref[...]) * gate
    y_ref[...] = _rms(h3, gf_ref[...])


def _ple(h2, p2d, w_ple, g_ple, w_gate, g_final):
    T = h2.shape[0]
    tb = _block(T, 512)
    tok = lambda w: pl.BlockSpec((tb, w), lambda i: (i, 0))
    full = lambda shape: pl.BlockSpec(shape, lambda i: (0, 0))
    return pl.pallas_call(
        _ple_kernel,
        out_shape=jax.ShapeDtypeStruct((T, D_MODEL), F32),
        grid=(T // tb,),
        in_specs=[pl.BlockSpec((tb, ROW_TILES, LANES), lambda i: (i, 0, 0)),
                  tok(PLE_DIM), full((PLE_DIM, D_MODEL)), full((1, D_MODEL)),
                  full((D_MODEL, D_MODEL)), full((1, D_MODEL))],
        out_specs=tok(D_MODEL),
        compiler_params=_params(
            ("arbitrary",), _f32_bytes((tb, D_MODEL)) * 2 + _f32_bytes((tb, PLE_DIM))
            + 2 * (PLE_DIM * D_MODEL + D_MODEL * D_MODEL)),
        name="ple",
    )(h2, p2d, w_ple.astype(BF16), g_ple.reshape(1, D_MODEL), w_gate.astype(BF16),
      g_final.reshape(1, D_MODEL))


def _run_stream(h, p, s0, r0, c0, lw, g_final, lower_bounds, layer):
    B, L, _ = h.shape
    T = B * L
    h1, xn, q, s_new, r_new, c_new = _mixer(h, lw, lower_bounds, s0, c0, r0, layer)
    eidx, gates = _route(q, lw["peer_keys"])
    h2 = _gather(eidx, gates, xn, h1, lw["peer_u"], lw["peer_v"])
    y = _ple(h2, p.reshape(T, PLE_DIM), lw["w_ple"], lw["g_ple"], lw["w_ple_gate"], g_final)
    return y.reshape(B, L, D_MODEL), s_new, r_new, c_new


def kernel(x_prompt, x_sample, state_hgrn, state_rglru, state_conv, p_prompt, p_sample, g_mix, w_in, lower_bounds, g_hgrn_norm, conv_w, conv_b, w_rg_a, b_rg_a, w_rg_x, b_rg_x, lambda_rg, w_out, g_ffn, w_peer_q, peer_keys, peer_u, peer_v, g_ple, w_ple_gate, w_ple, g_final):
    depth = w_in.shape[0]
    assert depth == 1, "the final norm is fused into the last layer's kernel; depth 1 only"
    layer = 0
    lw = dict(g_mix=g_mix[layer], w_in=w_in[layer], g_hgrn_norm=g_hgrn_norm[layer],
              conv_w=conv_w[layer], conv_b=conv_b[layer], w_rg_a=w_rg_a[layer], b_rg_a=b_rg_a[layer],
              w_rg_x=w_rg_x[layer], b_rg_x=b_rg_x[layer], lambda_rg=lambda_rg[layer],
              w_out=w_out[layer], g_ffn=g_ffn[layer], w_peer_q=w_peer_q[layer],
              peer_keys=peer_keys[layer], peer_u=peer_u[layer], peer_v=peer_v[layer],
              g_ple=g_ple[layer], w_ple_gate=w_ple_gate[layer], w_ple=w_ple[layer])
    bp = x_prompt.shape[0]
    y_p, s_p, r_p, c_p = _run_stream(
        x_prompt, p_prompt[layer], jnp.zeros((bp, H_A, DK, DV), F32), jnp.zeros((bp, W_B), F32),
        jnp.zeros((bp, CONV_W - 1, W_B), F32), lw, g_final, lower_bounds, layer)
    y_s, s_s, r_s, c_s = _run_stream(
        x_sample, p_sample[layer], state_hgrn[layer], state_rglru[layer], state_conv[layer],
        lw, g_final, lower_bounds, layer)
    return (y_p, y_s, s_p[None], r_p[None], c_p[None], s_s[None], r_s[None], c_s[None])
```
